```python
import jax
import jax.numpy as jnp
from jax import lax
import numpy as np


D_MODEL = 1024
BATCH = 4
SEQ = 4096
DEPTH = 4
DEC_BATCH = 4
DEC_SEQ = 8192
PAST_LEN = 128

GRID_W = 64
PLE_DIM = 256
EPS = 1e-6
POOL_WIDTH = 512
POOL_GROUPS = 4
POOL_GDIM = POOL_WIDTH // POOL_GROUPS
POOL_WINDOWS = (2, 4, 8, 16)
HG_HEADS = 4
HG_DK = 128
HG_DV = 128
HG_KW = HG_HEADS * HG_DK
HG_VW = HG_HEADS * HG_DV
HG_CHUNK = 64
ATT_HEADS = 16
ATT_KV_HEADS = 4
HEAD_DIM = 64
ATT_GROUP = ATT_HEADS // ATT_KV_HEADS
ATT_QW = ATT_HEADS * HEAD_DIM
ATT_KVW = ATT_KV_HEADS * HEAD_DIM
Q_BLOCK = 128
ROPE_THETA = 10000.0
ROPE_HALF = HEAD_DIM // 2
N_BRANCH = 3
D_FF = 2816
CONV_W = 3

OFF_POOL = 0
OFF_HQ = OFF_POOL + POOL_WIDTH
OFF_HFF = OFF_HQ + HG_KW
OFF_HFB = OFF_HFF + HG_KW
OFF_HI = OFF_HFB + HG_KW
OFF_HG = OFF_HI + HG_VW
OFF_AQ = OFF_HG + HG_VW
OFF_AK = OFF_AQ + ATT_QW
OFF_AV = OFF_AK + ATT_KVW
OFF_GATE = OFF_AV + ATT_KVW
IN_WIDTH = OFF_GATE + N_BRANCH * D_MODEL

kernel_name = 'hybrid_pool_hgrn2_gqa_encoder'


def _rmsnorm(x, g):
    xf = x.astype(jnp.float32)
    y = xf * lax.rsqrt(jnp.mean(xf * xf, axis=-1, keepdims=True) + EPS)
    return (y * g.astype(jnp.float32)).astype(x.dtype)


def _pool_mixer(u, w_grp, scale):
    b, n, _ = u.shape
    ug = u.reshape(b, n, POOL_GROUPS, POOL_GDIM).astype(jnp.float32)
    cs = jnp.concatenate([jnp.zeros((b, 1, POOL_GROUPS, POOL_GDIM), jnp.float32), jnp.cumsum(ug, axis=1)], axis=1)
    t = jnp.arange(n)[:, None]
    half = jnp.array([w // 2 for w in POOL_WINDOWS], jnp.int32)[None, :]
    lo = jnp.clip(t - half, 0, n)
    hi = jnp.clip(t + half, 0, n)
    gi = jnp.arange(POOL_GROUPS)[None, :]
    win_sum = cs[:, hi, gi, :] - cs[:, lo, gi, :]
    cnt = (hi - lo).astype(jnp.float32)[None, :, :, None]
    mixed = (win_sum / cnt - ug).astype(u.dtype)
    y = jnp.einsum('bngc,gcd->bngd', mixed, w_grp) * scale.reshape(POOL_GROUPS, POOL_GDIM)
    return y.reshape(b, n, POOL_WIDTH)


def _lower_bound(lb_raw):
    s = jnp.cumsum(jax.nn.softmax(lb_raw.astype(jnp.float32), axis=0), axis=0)
    return s - s[0:1]


def _gla_chunk_scan(q, k, v, logf):
    b, n, h, dk = q.shape
    dv = v.shape[-1]
    nc = n // HG_CHUNK

    def chunks(a):
        return a.reshape(b, nc, HG_CHUNK, h, a.shape[-1]).transpose(1, 0, 3, 2, 4)

    causal = jnp.tril(jnp.ones((HG_CHUNK, HG_CHUNK), bool))[:, :, None]

    def step(state, xs):
        qc, kc, vc, gc = xs
        cum = jnp.cumsum(gc, axis=2)
        rel = jnp.where(causal, cum[:, :, :, None, :] - cum[:, :, None, :, :], -jnp.inf)
        scores = jnp.einsum('bhjc,bhjlc,bhlc->bhjl', qc, jnp.exp(rel), kc)
        out = jnp.einsum('bhjl,bhlv->bhjv', scores, vc) + jnp.einsum('bhjc,bhcv->bhjv', qc * jnp.exp(cum), state)
        last = cum[:, :, -1:, :]
        state = jnp.exp(last[:, :, 0, :])[..., None] * state + jnp.einsum('bhlc,bhlv->bhcv', kc * jnp.exp(last - cum), vc)
        return state, out

    s0 = jnp.zeros((b, h, dk, dv), jnp.float32)
    _, out = lax.scan(step, s0, (chunks(q), chunks(k), chunks(v), chunks(logf)))
    return out.transpose(1, 0, 3, 2, 4).reshape(b, n, h, dv)


def _hgrn2_mixer(zq, zff, zfb, zi, zg, lb_f, lb_b, g_onorm):
    b, n, _ = zq.shape

    def heads(a, d):
        return a.astype(jnp.float32).reshape(b, n, HG_HEADS, d)

    q = heads(zq, HG_DK) * HG_DK ** -0.5
    v = heads(zi, HG_DV)

    def gates(z, lb):
        lbh = lb.reshape(HG_HEADS, HG_DK)
        logf = jnp.logaddexp(jnp.log(lbh), jnp.log1p(-lbh) + jax.nn.log_sigmoid(z))
        k = (1.0 - lbh) * jax.nn.sigmoid(-z)
        return k, logf

    k_fw, logf_fw = gates(heads(zff, HG_DK), lb_f)
    k_bw, logf_bw = gates(heads(zfb, HG_DK), lb_b)
    o_fw = _gla_chunk_scan(q, k_fw, v, logf_fw)

    def flip(a):
        return jnp.flip(a, axis=1)

    o_bw = flip(_gla_chunk_scan(flip(q), flip(k_bw), flip(v), flip(logf_bw)))
    o = _rmsnorm(o_fw + o_bw, g_onorm).reshape(b, n, HG_VW) * jax.nn.silu(zg.astype(jnp.float32))
    return o.astype(zq.dtype)


def _axial_rope_tables(n):
    rows = n // GRID_W
    row = jnp.repeat(jnp.arange(rows, dtype=jnp.float32), GRID_W)
    col = jnp.tile(jnp.arange(GRID_W, dtype=jnp.float32), rows)
    inv_freq = 1.0 / (ROPE_THETA ** (jnp.arange(0, ROPE_HALF, 2, dtype=jnp.float32) / ROPE_HALF))
    ang = jnp.concatenate([row[:, None] * inv_freq, col[:, None] * inv_freq], axis=-1)
    return jnp.cos(ang), jnp.sin(ang)


def _apply_axial_rope(x, cos, sin):
    b, n, h, d = x.shape
    xs = x.reshape(b, n, h, 2, 2, ROPE_HALF // 2)
    c = cos.reshape(n, 1, 2, ROPE_HALF // 2)
    s = sin.reshape(n, 1, 2, ROPE_HALF // 2)
    x1 = xs[..., 0, :]
    x2 = xs[..., 1, :]
    y1 = x1 * c - x2 * s
    y2 = x2 * c + x1 * s
    return jnp.stack([y1, y2], axis=-2).reshape(b, n, h, d)


def _gqa_axial(zq, zk, zv, g_q, g_k, cos, sin):
    b, n, _ = zq.shape
    dt = zq.dtype
    q = _rmsnorm(zq.reshape(b, n, ATT_HEADS, HEAD_DIM), g_q).astype(jnp.float32)
    k = _rmsnorm(zk.reshape(b, n, ATT_KV_HEADS, HEAD_DIM), g_k).astype(jnp.float32)
    q = (_apply_axial_rope(q, cos, sin) * HEAD_DIM ** -0.5).astype(dt)
    k = _apply_axial_rope(k, cos, sin).astype(dt)
    v = zv.reshape(b, n, ATT_KV_HEADS, HEAD_DIM)
    nb = n // Q_BLOCK
    qb = q.reshape(b, nb, Q_BLOCK, ATT_KV_HEADS, ATT_GROUP, HEAD_DIM).transpose(1, 0, 3, 4, 2, 5)
    kt = k.transpose(0, 2, 1, 3)
    vt = v.transpose(0, 2, 1, 3)

    def block(qblk):
        s = jnp.einsum('bkgqd,bknd->bkgqn', qblk, kt, preferred_element_type=jnp.float32)
        p = jax.nn.softmax(s, axis=-1)
        return jnp.einsum('bkgqn,bknd->bkgqd', p.astype(dt), vt)

    o = lax.map(block, qb)
    return o.transpose(1, 0, 4, 2, 3, 5).reshape(b, n, ATT_QW)


def _conv_glu_ffn(h, w_up, conv_w, conv_b, w_down):
    u = h @ w_up
    up = jnp.pad(u, ((0, 0), (1, 1), (0, 0)))
    c = up[:, :-2] * conv_w[0] + up[:, 1:-1] * conv_w[1] + up[:, 2:] * conv_w[2] + conv_b
    a, g = jnp.split(c, 2, axis=-1)
    return (jax.nn.gelu(a, approximate=True) * g) @ w_down


def _trunk(x, p, cos, sin, g_mix, w_in, pool_w, pool_scale, lb_f, lb_b, hg_onorm, g_q, g_k,
           w_br_pool, w_br_hg, w_br_att, w_out, g_ffn, w_up, conv_w, conv_b, w_down,
           g_ple, w_ple_gate, w_ple, g_final):
    for i in range(DEPTH):
        h = _rmsnorm(x, g_mix[i])
        z = h @ w_in[i]
        y_pool = _pool_mixer(z[..., OFF_POOL:OFF_HQ], pool_w[i], pool_scale[i])
        y_hg = _hgrn2_mixer(z[..., OFF_HQ:OFF_HFF], z[..., OFF_HFF:OFF_HFB], z[..., OFF_HFB:OFF_HI],
                            z[..., OFF_HI:OFF_HG], z[..., OFF_HG:OFF_AQ], lb_f[i], lb_b[i], hg_onorm[i])
        y_att = _gqa_axial(z[..., OFF_AQ:OFF_AK], z[..., OFF_AK:OFF_AV], z[..., OFF_AV:OFF_GATE],
                           g_q[i], g_k[i], cos, sin)
        gates = jax.nn.sigmoid(z[..., OFF_GATE:].astype(jnp.float32)).astype(x.dtype)
        gates = gates.reshape(z.shape[0], z.shape[1], N_BRANCH, D_MODEL)
        merged = (gates[..., 0, :] * (y_pool @ w_br_pool[i])
                  + gates[..., 1, :] * (y_hg @ w_br_hg[i])
                  + gates[..., 2, :] * (y_att @ w_br_att[i]))
        x = x + merged @ w_out[i]
        x = x + _conv_glu_ffn(_rmsnorm(x, g_ffn[i]), w_up[i], conv_w[i], conv_b[i], w_down[i])
        ple_gate = jax.nn.sigmoid(_rmsnorm(x, g_ple[i]) @ w_ple_gate[i])
        x = x + ple_gate * (p[i] @ w_ple[i])
    return _rmsnorm(x, g_final)


def setup_inputs(seed: int = 0) -> dict:
    key = jax.random.key(seed)
    ks = iter(jax.random.split(key, 32))

    def nrm(shape, scale):
        return jax.random.normal(next(ks), shape, jnp.float32) * scale

    def gain(shape):
        return 1.0 + nrm(shape, 0.02)

    return {
        'x_prompt': nrm((BATCH, SEQ, D_MODEL), 1.0),
        'x_sample': nrm((DEC_BATCH, DEC_SEQ, D_MODEL), 1.0),
        'p_prompt': nrm((DEPTH, BATCH, SEQ, PLE_DIM), 1.0),
        'p_sample': nrm((DEPTH, DEC_BATCH, DEC_SEQ, PLE_DIM), 1.0),
        'g_mix': gain((DEPTH, D_MODEL)),
        'w_in': nrm((DEPTH, D_MODEL, IN_WIDTH), D_MODEL ** -0.5),
        'pool_w': nrm((DEPTH, POOL_GROUPS, POOL_GDIM, POOL_GDIM), POOL_GDIM ** -0.5),
        'pool_scale': 1.0 + nrm((DEPTH, POOL_WIDTH), 0.1),
        'lb_raw_f': nrm((DEPTH, HG_KW), 0.1),
        'lb_raw_b': nrm((DEPTH, HG_KW), 0.1),
        'hg_onorm': gain((DEPTH, HG_DV)),
        'g_q': gain((DEPTH, HEAD_DIM)),
        'g_k': gain((DEPTH, HEAD_DIM)),
        'w_br_pool': nrm((DEPTH, POOL_WIDTH, D_MODEL), POOL_WIDTH ** -0.5),
        'w_br_hg': nrm((DEPTH, HG_VW, D_MODEL), HG_VW ** -0.5),
        'w_br_att': nrm((DEPTH, ATT_QW, D_MODEL), ATT_QW ** -0.5),
        'w_out': nrm((DEPTH, D_MODEL, D_MODEL), D_MODEL ** -0.5),
        'g_ffn': gain((DEPTH, D_MODEL)),
        'w_up': nrm((DEPTH, D_MODEL, 2 * D_FF), D_MODEL ** -0.5),
        'conv_w': nrm((DEPTH, CONV_W, 2 * D_FF), CONV_W ** -0.5),
        'conv_b': nrm((DEPTH, 2 * D_FF), 0.01),
        'w_down': nrm((DEPTH, D_FF, D_MODEL), D_FF ** -0.5),
        'g_ple': gain((DEPTH, D_MODEL)),
        'w_ple_gate': nrm((DEPTH, D_MODEL, D_MODEL), D_MODEL ** -0.5),
        'w_ple': nrm((DEPTH, PLE_DIM, D_MODEL), PLE_DIM ** -0.5),
        'g_final': gain((D_MODEL,)),
    }


def reference(x_prompt, x_sample, p_prompt, p_sample, g_mix, w_in, pool_w, pool_scale, lb_raw_f, lb_raw_b,
              hg_onorm, g_q, g_k, w_br_pool, w_br_hg, w_br_att, w_out, g_ffn, w_up, conv_w, conv_b, w_down,
              g_ple, w_ple_gate, w_ple, g_final):
    lb_f = _lower_bound(lb_raw_f)
    lb_b = _lower_bound(lb_raw_b)
    weights = (g_mix, w_in, pool_w, pool_scale, lb_f, lb_b, hg_onorm, g_q, g_k, w_br_pool, w_br_hg, w_br_att,
               w_out, g_ffn, w_up, conv_w, conv_b, w_down, g_ple, w_ple_gate, w_ple, g_final)
    cos_p, sin_p = _axial_rope_tables(x_prompt.shape[1])
    cos_s, sin_s = _axial_rope_tables(x_sample.shape[1])
    y_prompt = _trunk(x_prompt, p_prompt, cos_p, sin_p, *weights)
    y_sample = _trunk(x_sample, p_sample, cos_s, sin_s, *weights)
    return (y_prompt, y_sample)
```

```python
import functools

import jax
import jax.numpy as jnp
from jax import lax
from jax.experimental import pallas as pl
from jax.experimental.pallas import tpu as pltpu

F32 = jnp.float32
BF16 = jnp.bfloat16

EPS = 1e-6
D_MODEL = 1024
GRID_W = 64
PLE_DIM = 256
POOL_WIDTH = 512
POOL_GROUPS = 4
POOL_GDIM = POOL_WIDTH // POOL_GROUPS
POOL_HALF = (1, 2, 4, 8)
POOL_HALO = 8
HG_HEADS = 4
HG_DK = 128
HG_DV = 128
HG_W = HG_HEADS * HG_DK
HG_CHUNK = 64
HG_SUB = 16
ATT_HEADS = 16
ATT_KV_HEADS = 4
HEAD_DIM = 64
ATT_QW = ATT_HEADS * HEAD_DIM
ATT_KVW = ATT_KV_HEADS * HEAD_DIM
ROPE_THETA = 10000.0
ROPE_HALF = HEAD_DIM // 2
D_FF = 2816
FFN_HALO = 8

OFF_POOL = 0
OFF_HQ = 512
OFF_HFF = 1024
OFF_HFB = 1536
OFF_HI = 2048
OFF_HG = 2560
OFF_AQ = 3072
OFF_AK = 4096
OFF_AV = 4352
OFF_GATE = 4608
IN_WIDTH = 7680
GATE_BLK = 1536

LANES = 128
VMEM_LIMIT = 56 * 1024 * 1024


def _cparams(*sem):
    return pltpu.CompilerParams(dimension_semantics=sem, vmem_limit_bytes=VMEM_LIMIT)


def _tile(n, pref):
    t = min(n, pref)
    while n % t:
        t //= 2
    return t


def _dot(a, b):
    return jnp.dot(a, b, preferred_element_type=F32)


def _dot_nt(a, b):
    return lax.dot_general(a, b, (((1,), (1,)), ((), ())), preferred_element_type=F32)


def _dot_tn(a, b):
    return lax.dot_general(a, b, (((0,), (0,)), ((), ())), preferred_element_type=F32)


def _rms(x, g):
    return x * lax.rsqrt(jnp.mean(x * x, axis=-1, keepdims=True) + EPS) * g


def _norm_matmul_kernel(x_ref, g_ref, w_ref, o_ref, h_ref):
    @pl.when(pl.program_id(1) == 0)
    def _():
        h_ref[...] = _rms(x_ref[...], g_ref[...]).astype(BF16)

    o_ref[...] = _dot(h_ref[...], w_ref[...]).astype(o_ref.dtype)


def _norm_matmul(x, g, w, out_dtype=F32, tm_pref=1024, tn_pref=512):
    t, k = x.shape
    nout = w.shape[1]
    tm = _tile(t, tm_pref)
    tn = _tile(nout, tn_pref)
    return pl.pallas_call(
        _norm_matmul_kernel,
        grid=(t // tm, nout // tn),
        in_specs=[
            pl.BlockSpec((tm, k), lambda i, j: (i, 0)),
            pl.BlockSpec((1, k), lambda i, j: (0, 0)),
            pl.BlockSpec((k, tn), lambda i, j: (0, j)),
        ],
        out_specs=pl.BlockSpec((tm, tn), lambda i, j: (i, j)),
        out_shape=jax.ShapeDtypeStruct((t, nout), out_dtype),
        scratch_shapes=[pltpu.VMEM((tm, k), BF16)],
        compiler_params=_cparams("parallel", "arbitrary"),
        name="norm_in_proj",
    )(x, g, w)


def _pool_kernel(cur_ref, prev_ref, next_ref, w_ref, scale_ref, o_ref, *, n, tn):
    i = pl.program_id(1)
    t0 = i * tn
    u = cur_ref[0]
    prev = jnp.where(i > 0, prev_ref[0], 0.0)
    nxt = jnp.where(t0 + tn < n, next_ref[0], 0.0)
    e = jnp.concatenate([prev, u, nxt], axis=0)
    s2 = e[0:-1] + e[1:]
    c1 = POOL_GDIM
    s4 = s2[0:-2, c1:] + s2[2:, c1:]
    s8 = s4[0:-4, c1:] + s4[4:, c1:]
    s16 = s8[0:-8, c1:] + s8[8:, c1:]
    sums = (s2[7:7 + tn, :c1], s4[6:6 + tn, :c1], s8[4:4 + tn, :c1], s16[0:tn, :])
    t = t0 + lax.broadcasted_iota(jnp.int32, (tn, 1), 0)
    outs = []
    for g in range(POOL_GROUPS):
        h = POOL_HALF[g]
        cnt = (jnp.minimum(t + h, n) - jnp.maximum(t - h, 0)).astype(F32)
        ug = u[:, g * c1:(g + 1) * c1]
        mixed = (sums[g] / cnt - ug).astype(BF16)
        outs.append(_dot(mixed, w_ref[g]))
    y = jnp.concatenate(outs, axis=1) * scale_ref[...]
    o_ref[0] = y.astype(o_ref.dtype)


def _pool_mixer(z, pool_w, pool_scale, tn_pref=512):
    b, n, _ = z.shape
    tn = _tile(n, tn_pref)
    hb = tn // POOL_HALO
    nhb = n // POOL_HALO
    return pl.pallas_call(
        functools.partial(_pool_kernel, n=n, tn=tn),
        grid=(b, n // tn),
        in_specs=[
            pl.BlockSpec((1, tn, POOL_WIDTH), lambda bi, i: (bi, i, OFF_POOL // POOL_WIDTH)),
            pl.BlockSpec((1, POOL_HALO, POOL_WIDTH),
                         lambda bi, i: (bi, jnp.maximum(i * hb - 1, 0), OFF_POOL // POOL_WIDTH)),
            pl.BlockSpec((1, POOL_HALO, POOL_WIDTH),
                         lambda bi, i: (bi, jnp.minimum((i + 1) * hb, nhb - 1), OFF_POOL // POOL_WIDTH)),
            pl.BlockSpec((POOL_GROUPS, POOL_GDIM, POOL_GDIM), lambda bi, i: (0, 0, 0)),
            pl.BlockSpec((1, POOL_WIDTH), lambda bi, i: (0, 0)),
        ],
        out_specs=pl.BlockSpec((1, tn, POOL_WIDTH), lambda bi, i: (bi, i, 0)),
        out_shape=jax.ShapeDtypeStruct((b, n, POOL_WIDTH), BF16),
        compiler_params=_cparams("parallel", "parallel"),
        name="pool_mixer",
    )(z, z, z, pool_w, pool_scale)


def _split3(x):
    hi = x.astype(BF16)
    r1 = x - hi.astype(F32)
    mid = r1.astype(BF16)
    lo = (r1 - mid.astype(F32)).astype(BF16)
    return hi, mid, lo


def _hgrn_chunk(zq, zf, v, log_lb, log1m_lb, one_m_lb, st_ref, rev):
    c = HG_CHUNK
    q = zq * (HG_DK ** -0.5)
    ez = jnp.exp(-jnp.abs(zf))
    log_sig = jnp.minimum(zf, 0.0) - jnp.log1p(ez)
    k = one_m_lb * (jnp.where(zf >= 0.0, ez, 1.0) / (1.0 + ez))
    ga = log_lb
    gb = log1m_lb + log_sig
    logf = jnp.maximum(ga, gb) + jnp.log1p(jnp.exp(-jnp.abs(ga - gb)))

    row = lax.broadcasted_iota(jnp.int32, (c, c), 0)
    col = lax.broadcasted_iota(jnp.int32, (c, c), 1)
    keep = (col >= row) if rev else (col <= row)
    tri = jnp.where(keep, 1.0, 0.0).astype(BF16)
    hi, mid, lo = _split3(logf)
    cum = _dot(tri, hi) + _dot(tri, mid) + _dot(tri, lo)
    last = cum[0:1] if rev else cum[c - 1:c]
    qhat = (q * jnp.exp(cum)).astype(BF16)
    khat = (k * jnp.exp(last - cum)).astype(BF16)
    dec = jnp.exp(last)

    nsub = c // HG_SUB
    a_rows = [[] for _ in range(HG_HEADS)]
    col_s = lax.broadcasted_iota(jnp.int32, (HG_SUB, c), 1)
    for jb in range(nsub):
        r0 = jb * HG_SUB
        qs = q[r0:r0 + HG_SUB]
        cs = cum[r0:r0 + HG_SUB]
        has_off = (jb < nsub - 1) if rev else (jb > 0)
        if has_off:
            ridx = r0 + HG_SUB if rev else r0 - 1
            ref_row = cum[ridx:ridx + 1]
            qt = (qs * jnp.exp(cs - ref_row)).astype(BF16)
            kt = (k * jnp.exp(jnp.minimum(ref_row - cum, 0.0))).astype(BF16)
        blocks = []
        for h in range(HG_HEADS):
            sl = slice(h * HG_DK, (h + 1) * HG_DK)
            if has_off:
                blocks.append(_dot_nt(qt[:, sl], kt[:, sl]))
            else:
                blocks.append(jnp.zeros((HG_SUB, c), F32))
        for l in range(HG_SUB):
            lr = r0 + l
            x = qs * jnp.exp(jnp.minimum(cs - cum[lr:lr + 1], 0.0)) * k[lr:lr + 1]
            for h in range(HG_HEADS):
                s = jnp.sum(x[:, h * HG_DK:(h + 1) * HG_DK], axis=-1, keepdims=True)
                blocks[h] = jnp.where(col_s == lr, s, blocks[h])
        for h in range(HG_HEADS):
            a_rows[h].append(blocks[h])

    outs = []
    for h in range(HG_HEADS):
        sl = slice(h * HG_DK, (h + 1) * HG_DK)
        a = jnp.where(keep, jnp.concatenate(a_rows[h], axis=0), 0.0).astype(BF16)
        vh = v[:, sl].astype(BF16)
        st = st_ref[h]
        outs.append(_dot(a, vh) + _dot_nt(qhat[:, sl], st.astype(BF16)))
        st_ref[h] = st * dec[:, sl] + _dot_tn(vh, khat[:, sl])
    return jnp.concatenate(outs, axis=1)


def _hgrn_fw_kernel(zq_ref, zf_ref, zi_ref, llb_ref, l1m_ref, oml_ref, o_ref, st_ref):
    @pl.when(pl.program_id(1) == 0)
    def _():
        st_ref[...] = jnp.zeros_like(st_ref)

    o_ref[0] = _hgrn_chunk(zq_ref[0], zf_ref[0], zi_ref[0], llb_ref[...], l1m_ref[...], oml_ref[...],
                           st_ref, rev=False)


def _hgrn_bw_kernel(zq_ref, zf_ref, zi_ref, zg_ref, ofw_ref, llb_ref, l1m_ref, oml_ref, gon_ref, o_ref, st_ref):
    @pl.when(pl.program_id(1) == 0)
    def _():
        st_ref[...] = jnp.zeros_like(st_ref)

    o = ofw_ref[0] + _hgrn_chunk(zq_ref[0], zf_ref[0], zi_ref[0], llb_ref[...], l1m_ref[...], oml_ref[...],
                                 st_ref, rev=True)
    zg = zg_ref[0]
    gon = gon_ref[...]
    outs = []
    for h in range(HG_HEADS):
        sl = slice(h * HG_DV, (h + 1) * HG_DV)
        outs.append(_rms(o[:, sl], gon))
    y = jnp.concatenate(outs, axis=1) * (zg * jax.nn.sigmoid(zg))
    o_ref[0] = y.astype(o_ref.dtype)


def _hgrn2_mixer(z, lb_f, lb_b, g_onorm):
    b, n, _ = z.shape
    c = HG_CHUNK
    nc = n // c
    wblk = HG_W

    def col(off):
        return off // wblk

    def rows(lbv):
        lbv = lbv.reshape(1, HG_W).astype(F32)
        return jnp.log(lbv), jnp.log1p(-lbv), 1.0 - lbv

    def zspec(off, rev):
        if rev:
            return pl.BlockSpec((1, c, wblk), lambda bi, i: (bi, nc - 1 - i, col(off)))
        return pl.BlockSpec((1, c, wblk), lambda bi, i: (bi, i, col(off)))

    vec = pl.BlockSpec((1, HG_W), lambda bi, i: (0, 0))
    o_fw = pl.pallas_call(
        _hgrn_fw_kernel,
        grid=(b, nc),
        in_specs=[zspec(OFF_HQ, False), zspec(OFF_HFF, False), zspec(OFF_HI, False), vec, vec, vec],
        out_specs=pl.BlockSpec((1, c, HG_W), lambda bi, i: (bi, i, 0)),
        out_shape=jax.ShapeDtypeStruct((b, n, HG_W), F32),
        scratch_shapes=[pltpu.VMEM((HG_HEADS, HG_DV, HG_DK), F32)],
        compiler_params=_cparams("parallel", "arbitrary"),
        name="hgrn2_forward",
    )(z, z, z, *rows(lb_f))
    return pl.pallas_call(
        _hgrn_bw_kernel,
        grid=(b, nc),
        in_specs=[zspec(OFF_HQ, True), zspec(OFF_HFB, True), zspec(OFF_HI, True), zspec(OFF_HG, True),
                  pl.BlockSpec((1, c, HG_W), lambda bi, i: (bi, nc - 1 - i, 0)), vec, vec, vec,
                  pl.BlockSpec((1, HG_DV), lambda bi, i: (0, 0))],
        out_specs=pl.BlockSpec((1, c, HG_W), lambda bi, i: (bi, nc - 1 - i, 0)),
        out_shape=jax.ShapeDtypeStruct((b, n, HG_W), BF16),
        scratch_shapes=[pltpu.VMEM((HG_HEADS, HG_DV, HG_DK), F32)],
        compiler_params=_cparams("parallel", "arbitrary"),
        name="hgrn2_backward_combine",
    )(z, z, z, z, o_fw, *rows(lb_b), g_onorm.reshape(1, HG_DV).astype(F32))


def _headnorm_rope(x, avg, g, cosv, sinv):
    hi = (x * x).astype(BF16)
    lo = (x * x - hi.astype(F32)).astype(BF16)
    ms = _dot(hi, avg) + _dot(lo, avg)
    y = x * lax.rsqrt(ms + EPS) * g
    lane = lax.broadcasted_iota(jnp.int32, y.shape, 1)
    first = (lane % ROPE_HALF) < (ROPE_HALF // 2)
    partner = jnp.where(first, pltpu.roll(y, LANES - ROPE_HALF // 2, 1), pltpu.roll(y, ROPE_HALF // 2, 1))
    return y * cosv + partner * sinv


def _attn_prep_kernel(zq_ref, zk_ref, zv_ref, cos_ref, sin_ref, gq_ref, gk_ref, avg_ref, q_ref, kk_ref, v_ref):
    cosv = cos_ref[...]
    sinv = sin_ref[...]
    avg = avg_ref[...]
    for p in range(ATT_QW // LANES):
        sl = slice(p * LANES, (p + 1) * LANES)
        y = _headnorm_rope(zq_ref[0, :, sl], avg, gq_ref[...], cosv, sinv) * (HEAD_DIM ** -0.5)
        q_ref[0, :, sl] = y.astype(q_ref.dtype)
    lane = lax.broadcasted_iota(jnp.int32, cosv.shape, 1)
    low = lane < HEAD_DIM
    for p in range(ATT_KVW // LANES):
        sl = slice(p * LANES, (p + 1) * LANES)
        kt = _headnorm_rope(zk_ref[0, :, sl], avg, gk_ref[...], cosv, sinv).T.astype(kk_ref.dtype)
        for hh in range(2):
            kh = kt[hh * HEAD_DIM:(hh + 1) * HEAD_DIM]
            g = 2 * p + hh
            kk_ref[0, g * LANES:g * LANES + HEAD_DIM, :] = kh
            kk_ref[0, g * LANES + HEAD_DIM:(g + 1) * LANES, :] = kh
        v = zv_ref[0, :, sl]
        vs = pltpu.roll(v, HEAD_DIM, 1)
        for hh, (a, bsw) in enumerate(((v, vs), (vs, v))):
            g = 2 * p + hh
            v_ref[0, :, 2 * g * LANES:(2 * g + 1) * LANES] = jnp.where(low, a, 0.0).astype(v_ref.dtype)
            v_ref[0, :, (2 * g + 1) * LANES:(2 * g + 2) * LANES] = jnp.where(low, 0.0, bsw).astype(v_ref.dtype)


def _flash_kernel(q_ref, kk_ref, v_ref, o_ref, qm_ref, m_ref, l_ref, acc_ref):
    kv = pl.program_id(2)
    tq = q_ref.shape[1]
    lane = lax.broadcasted_iota(jnp.int32, (tq, LANES), 1)
    low = lane < HEAD_DIM

    @pl.when(kv == 0)
    def _():
        for p in range(ATT_HEADS // 2):
            slab = q_ref[0, :, p * LANES:(p + 1) * LANES]
            zero = jnp.zeros_like(slab)
            qm_ref[2 * p] = jnp.where(low, slab, zero)
            qm_ref[2 * p + 1] = jnp.where(low, zero, slab)
        m_ref[...] = jnp.full_like(m_ref, -jnp.inf)
        l_ref[...] = jnp.zeros_like(l_ref)
        acc_ref[...] = jnp.zeros_like(acc_ref)

    for g in range(ATT_KV_HEADS):
        kk = kk_ref[0, g * LANES:(g + 1) * LANES, :]
        vpad = (v_ref[0, :, 2 * g * LANES:(2 * g + 1) * LANES],
                v_ref[0, :, (2 * g + 1) * LANES:(2 * g + 2) * LANES])
        for pp in range(2):
            p = 2 * g + pp
            pv = None
            alphas = []
            for hh in range(2):
                h = 2 * p + hh
                s = _dot(qm_ref[h], kk)
                m_prev = m_ref[h]
                m_new = jnp.maximum(m_prev, jnp.max(s, axis=-1, keepdims=True))
                alpha = jnp.exp(m_prev - m_new)
                pe = jnp.exp(s - m_new)
                l_ref[h] = alpha * l_ref[h] + jnp.sum(pe, axis=-1, keepdims=True)
                m_ref[h] = m_new
                d = _dot(pe.astype(BF16), vpad[hh])
                pv = d if pv is None else pv + d
                alphas.append(alpha)
            sl = slice(p * LANES, (p + 1) * LANES)
            acc_ref[:, sl] = acc_ref[:, sl] * jnp.where(low, alphas[0], alphas[1]) + pv

    @pl.when(kv == pl.num_programs(2) - 1)
    def _():
        for p in range(ATT_HEADS // 2):
            sl = slice(p * LANES, (p + 1) * LANES)
            inv = jnp.where(low, 1.0 / l_ref[2 * p], 1.0 / l_ref[2 * p + 1])
            o_ref[0, :, sl] = (acc_ref[:, sl] * inv).astype(o_ref.dtype)


def _rope_tables(n):
    rows = n // GRID_W
    row = jnp.repeat(jnp.arange(rows, dtype=F32), GRID_W)
    colp = jnp.tile(jnp.arange(GRID_W, dtype=F32), rows)
    inv_freq = 1.0 / (ROPE_THETA ** (jnp.arange(0, ROPE_HALF, 2, dtype=F32) / ROPE_HALF))
    ang_r = row[:, None] * inv_freq
    ang_c = colp[:, None] * inv_freq
    cos64 = jnp.concatenate([jnp.cos(ang_r), jnp.cos(ang_r), jnp.cos(ang_c), jnp.cos(ang_c)], axis=-1)
    sin64 = jnp.concatenate([-jnp.sin(ang_r), jnp.sin(ang_r), -jnp.sin(ang_c), jnp.sin(ang_c)], axis=-1)
    return jnp.tile(cos64, (1, 2)), jnp.tile(sin64, (1, 2))


def _gqa_axial(z, g_q, g_k, cos_t, sin_t, tm_pref=512, tq_pref=512, tk_pref=512):
    b, n, _ = z.shape
    tm = _tile(n, tm_pref)
    seg = lax.broadcasted_iota(jnp.int32, (LANES, LANES), 0) // HEAD_DIM
    seg_t = lax.broadcasted_iota(jnp.int32, (LANES, LANES), 1) // HEAD_DIM
    avg = jnp.where(seg == seg_t, 1.0 / HEAD_DIM, 0.0).astype(BF16)
    gq2 = jnp.tile(g_q.reshape(1, HEAD_DIM).astype(F32), (1, 2))
    gk2 = jnp.tile(g_k.reshape(1, HEAD_DIM).astype(F32), (1, 2))
    vec = pl.BlockSpec((1, LANES), lambda bi, i: (0, 0))
    tab = pl.BlockSpec((tm, LANES), lambda bi, i: (i, 0))
    qh, kk, vx = pl.pallas_call(
        _attn_prep_kernel,
        grid=(b, n // tm),
        in_specs=[
            pl.BlockSpec((1, tm, ATT_QW), lambda bi, i: (bi, i, OFF_AQ // ATT_QW)),
            pl.BlockSpec((1, tm, ATT_KVW), lambda bi, i: (bi, i, OFF_AK // ATT_KVW)),
            pl.BlockSpec((1, tm, ATT_KVW), lambda bi, i: (bi, i, OFF_AV // ATT_KVW)),
            tab, tab, vec, vec,
            pl.BlockSpec((LANES, LANES), lambda bi, i: (0, 0)),
        ],
        out_specs=[
            pl.BlockSpec((1, tm, ATT_QW), lambda bi, i: (bi, i, 0)),
            pl.BlockSpec((1, ATT_KV_HEADS * LANES, tm), lambda bi, i: (bi, 0, i)),
            pl.BlockSpec((1, tm, 2 * ATT_KV_HEADS * LANES), lambda bi, i: (bi, i, 0)),
        ],
        out_shape=[
            jax.ShapeDtypeStruct((b, n, ATT_QW), BF16),
            jax.ShapeDtypeStruct((b, ATT_KV_HEADS * LANES, n), BF16),
            jax.ShapeDtypeStruct((b, n, 2 * ATT_KV_HEADS * LANES), BF16),
        ],
        compiler_params=_cparams("parallel", "parallel"),
        name="attn_prep",
    )(z, z, z, cos_t, sin_t, gq2, gk2, avg)

    tq = _tile(n, tq_pref)
    tk = _tile(n, tk_pref)
    return pl.pallas_call(
        _flash_kernel,
        grid=(b, n // tq, n // tk),
        in_specs=[
            pl.BlockSpec((1, tq, ATT_QW), lambda bi, i, j: (bi, i, 0)),
            pl.BlockSpec((1, ATT_KV_HEADS * LANES, tk), lambda bi, i, j: (bi, 0, j)),
            pl.BlockSpec((1, tk, 2 * ATT_KV_HEADS * LANES), lambda bi, i, j: (bi, j, 0)),
        ],
        out_specs=pl.BlockSpec((1, tq, ATT_QW), lambda bi, i, j: (bi, i, 0)),
        out_shape=jax.ShapeDtypeStruct((b, n, ATT_QW), BF16),
        scratch_shapes=[
            pltpu.VMEM((ATT_HEADS, tq, LANES), BF16),
            pltpu.VMEM((ATT_HEADS, tq, 1), F32),
            pltpu.VMEM((ATT_HEADS, tq, 1), F32),
            pltpu.VMEM((tq, ATT_QW), F32),
        ],
        compiler_params=_cparams("parallel", "parallel", "arbitrary"),
        name="flash_gqa",
    )(qh, kk, vx)


def _merge_kernel(x_ref, yp_ref, yh_ref, ya_ref, ga_ref, gb_ref, wp_ref, wh_ref, wa_ref, wo_ref, o_ref):
    ga = jax.nn.sigmoid(ga_ref[...])
    gb = jax.nn.sigmoid(gb_ref[...])
    d = D_MODEL
    g_pool = ga[:, :d]
    g_hg = jnp.concatenate([ga[:, d:], gb[:, :2 * d - GATE_BLK]], axis=1)
    g_att = gb[:, 2 * d - GATE_BLK:]
    merged = (g_pool * _dot(yp_ref[...], wp_ref[...])
              + g_hg * _dot(yh_ref[...], wh_ref[...])
              + g_att * _dot(ya_ref[...], wa_ref[...]))
    o_ref[...] = x_ref[...] + _dot(merged.astype(BF16), wo_ref[...])


def _merge(x, y_pool, y_hg, y_att, z2, w_p, w_h, w_a, w_o, tm_pref=512):
    t, d = x.shape
    tm = _tile(t, tm_pref)

    def rowblk(w):
        return pl.BlockSpec((tm, w), lambda i: (i, 0))

    def whole(w):
        return pl.BlockSpec(w.shape, lambda i: (0, 0))

    return pl.pallas_call(
        _merge_kernel,
        grid=(t // tm,),
        in_specs=[
            rowblk(d), rowblk(POOL_WIDTH), rowblk(HG_W), rowblk(ATT_QW),
            pl.BlockSpec((tm, GATE_BLK), lambda i: (i, OFF_GATE // GATE_BLK)),
            pl.BlockSpec((tm, GATE_BLK), lambda i: (i, OFF_GATE // GATE_BLK + 1)),
            whole(w_p), whole(w_h), whole(w_a), whole(w_o),
        ],
        out_specs=rowblk(d),
        out_shape=jax.ShapeDtypeStruct((t, d), F32),
        compiler_params=_cparams("parallel"),
        name="merge_out_proj",
    )(x, y_pool, y_hg, y_att, z2, z2, w_p, w_h, w_a, w_o)


def _gelu_tanh(a):
    return 0.5 * a * (1.0 + jnp.tanh(0.7978845608028654 * (a + 0.044715 * (a * a * a))))


def _ffn_kernel(x_ref, prev_ref, next_ref, g_ref, wa_ref, wg_ref, cwa_ref, cwg_ref, cba_ref, cbg_ref, wd_ref,
                o_ref, h_ref, acc_ref, *, n, tm):
    i = pl.program_id(1)
    j = pl.program_id(2)
    hl = FFN_HALO

    @pl.when(j == 0)
    def _():
        g = g_ref[...]
        h_ref[hl:hl + tm] = _rms(x_ref[0], g).astype(BF16)
        h_ref[0:hl] = jnp.where(i > 0, _rms(prev_ref[0], g), 0.0).astype(BF16)
        h_ref[hl + tm:] = jnp.where((i + 1) * tm < n, _rms(next_ref[0], g), 0.0).astype(BF16)
        acc_ref[...] = jnp.zeros_like(acc_ref)

    h = h_ref[...]

    def conv(w_ref, cw_ref, cb_ref):
        u = _dot(h, w_ref[...])
        cw = cw_ref[...]
        return (u[hl - 1:hl - 1 + tm] * cw[0:1] + u[hl:hl + tm] * cw[1:2] + u[hl + 1:hl + 1 + tm] * cw[2:3]
                + cb_ref[...])

    act = _gelu_tanh(conv(wa_ref, cwa_ref, cba_ref)) * conv(wg_ref, cwg_ref, cbg_ref)
    acc_ref[...] += _dot(act.astype(BF16), wd_ref[...])

    @pl.when(j == pl.num_programs(2) - 1)
    def _():
        o_ref[0] = x_ref[0] + acc_ref[...]


def _ffn(x, g, w_up, conv_w, conv_b, w_down, tm_pref=512, tf_pref=256):
    b, n, d = x.shape
    tm = _tile(n, tm_pref)
    tf = _tile(D_FF, tf_pref)
    nf = D_FF // tf
    hb = tm // FFN_HALO
    nhb = n // FFN_HALO
    return pl.pallas_call(
        functools.partial(_ffn_kernel, n=n, tm=tm),
        grid=(b, n // tm, nf),
        in_specs=[
            pl.BlockSpec((1, tm, d), lambda bi, i, j: (bi, i, 0)),
            pl.BlockSpec((1, FFN_HALO, d), lambda bi, i, j: (bi, jnp.maximum(i * hb - 1, 0), 0)),
            pl.BlockSpec((1, FFN_HALO, d), lambda bi, i, j: (bi, jnp.minimum((i + 1) * hb, nhb - 1), 0)),
            pl.BlockSpec((1, d), lambda bi, i, j: (0, 0)),
            pl.BlockSpec((d, tf), lambda bi, i, j: (0, j)),
            pl.BlockSpec((d, tf), lambda bi, i, j: (0, nf + j)),
            pl.BlockSpec((3, tf), lambda bi, i, j: (0, j)),
            pl.BlockSpec((3, tf), lambda bi, i, j: (0, nf + j)),
            pl.BlockSpec((1, tf), lambda bi, i, j: (0, j)),
            pl.BlockSpec((1, tf), lambda bi, i, j: (0, nf + j)),
            pl.BlockSpec((tf, d), lambda bi, i, j: (j, 0)),
        ],
        out_specs=pl.BlockSpec((1, tm, d), lambda bi, i, j: (bi, i, 0)),
        out_shape=jax.ShapeDtypeStruct((b, n, d), F32),
        scratch_shapes=[pltpu.VMEM((tm + 2 * FFN_HALO, d), BF16), pltpu.VMEM((tm, d), F32)],
        compiler_params=_cparams("parallel", "parallel", "arbitrary"),
        name="conv_glu_ffn",
    )(x, x, x, g, w_up, w_up, conv_w, conv_w, conv_b, conv_b, w_down)


def _ple_kernel(x_ref, p_ref, g_ref, wg_ref, wp_ref, gf_ref, o_ref, *, final):
    x = x_ref[...]
    gate = jax.nn.sigmoid(_dot(_rms(x, g_ref[...]).astype(BF16), wg_ref[...]))
    y = x + gate * _dot(p_ref[...].astype(BF16), wp_ref[...])
    if final:
        y = _rms(y, gf_ref[...])
    o_ref[...] = y


def _ple(x, p, g, w_gate, w_ple, g_final, final, tm_pref=512):
    t, d = x.shape
    tm = _tile(t, tm_pref)
    return pl.pallas_call(
        functools.partial(_ple_kernel, final=final),
        grid=(t // tm,),
        in_specs=[
            pl.BlockSpec((tm, d), lambda i: (i, 0)),
            pl.BlockSpec((tm, PLE_DIM), lambda i: (i, 0)),
            pl.BlockSpec((1, d), lambda i: (0, 0)),
            pl.BlockSpec((d, d), lambda i: (0, 0)),
            pl.BlockSpec((PLE_DIM, d), lambda i: (0, 0)),
            pl.BlockSpec((1, d), lambda i: (0, 0)),
        ],
        out_specs=pl.BlockSpec((tm, d), lambda i: (i, 0)),
        out_shape=jax.ShapeDtypeStruct((t, d), F32),
        compiler_params=_cparams("parallel"),
        name="ple_final_norm" if final else "ple",
    )(x, p, g, w_gate, w_ple, g_final)


def _lower_bound(lb_raw):
    s = jnp.cumsum(jax.nn.softmax(lb_raw.astype(F32), axis=0), axis=0)
    return s - s[0:1]


def _trunk(x, p, wts):
    b, n, d = x.shape
    t = b * n
    depth = wts["w_in"].shape[0]
    cos_t, sin_t = _rope_tables(n)
    x2 = x.reshape(t, d)
    for i in range(depth):
        z2 = _norm_matmul(x2, wts["g_mix"][i], wts["w_in"][i])
        z = z2.reshape(b, n, IN_WIDTH)
        y_pool = _pool_mixer(z, wts["pool_w"][i], wts["pool_scale"][i])
        y_hg = _hgrn2_mixer(z, wts["lb_f"][i], wts["lb_b"][i], wts["hg_onorm"][i])
        y_att = _gqa_axial(z, wts["g_q"][i], wts["g_k"][i], cos_t, sin_t)
        x2 = _merge(x2, y_pool.reshape(t, POOL_WIDTH), y_hg.reshape(t, HG_W), y_att.reshape(t, ATT_QW), z2,
                    wts["w_br_pool"][i], wts["w_br_hg"][i], wts["w_br_att"][i], wts["w_out"][i])
        x2 = _ffn(x2.reshape(b, n, d), wts["g_ffn"][i], wts["w_up"][i], wts["conv_w"][i], wts["conv_b"][i],
                  wts["w_down"][i]).reshape(t, d)
        x2 = _ple(x2, p[i].reshape(t, PLE_DIM), wts["g_ple"][i], wts["w_ple_gate"][i], wts["w_ple"][i],
                  wts["g_final"], final=(i == depth - 1))
    return x2.reshape(b, n, d)


def kernel(x_prompt, x_sample, p_prompt, p_sample, g_mix, w_in, pool_w, pool_scale, lb_raw_f, lb_raw_b, hg_onorm,
           g_q, g_k, w_br_pool, w_br_hg, w_br_att, w_out, g_ffn, w_up, conv_w, conv_b, w_down, g_ple, w_ple_gate,
           w_ple, g_final):
    depth = w_in.shape[0]

    def vec(a):
        return a.astype(F32).reshape(depth, 1, a.shape[-1])

    wts = dict(
        g_mix=vec(g_mix), w_in=w_in.astype(BF16), pool_w=pool_w.astype(BF16), pool_scale=vec(pool_scale),
        lb_f=_lower_bound(lb_raw_f), lb_b=_lower_bound(lb_raw_b), hg_onorm=hg_onorm, g_q=g_q, g_k=g_k,
        w_br_pool=w_br_pool.astype(BF16), w_br_hg=w_br_hg.astype(BF16), w_br_att=w_br_att.astype(BF16),
        w_out=w_out.astype(BF16), g_ffn=vec(g_ffn), w_up=w_up.astype(BF16), conv_w=conv_w.astype(F32),
        conv_b=vec(conv_b), w_down=w_down.astype(BF16), g_ple=vec(g_ple), w_ple_gate=w_ple_gate.astype(BF16),
        w_ple=w_ple.astype(BF16), g_final=g_final.astype(F32).reshape(1, -1),
    )
    return _trunk(x_prompt, p_prompt, wts), _trunk(x_sample, p_sample, wts)
```

```python
import functools

import jax
import jax.numpy as jnp
from jax import lax
from jax.experimental import pallas as pl
from jax.experimental.pallas import tpu as pltpu

F32 = jnp.float32
BF16 = jnp.bfloat16

EPS = 1e-6
D_MODEL = 1024
GRID_W = 64
PLE_DIM = 256
POOL_WIDTH = 512
POOL_GROUPS = 4
POOL_GDIM = POOL_WIDTH // POOL_GROUPS
POOL_HALF = (1, 2, 4, 8)
POOL_HALO = 8
HG_HEADS = 4
HG_DK = 128
HG_DV = 128
HG_W = HG_HEADS * HG_DK
HG_CHUNK = 64
HG_SUB = 16
ATT_HEADS = 16
ATT_KV_HEADS = 4
HEAD_DIM = 64
ATT_QW = ATT_HEADS * HEAD_DIM
ATT_KVW = ATT_KV_HEADS * HEAD_DIM
ROPE_THETA = 10000.0
ROPE_HALF = HEAD_DIM // 2
D_FF = 2816
FFN_HALO = 8

OFF_POOL = 0
OFF_HQ = 512
OFF_HFF = 1024
OFF_HFB = 1536
OFF_HI = 2048
OFF_HG = 2560
OFF_AQ = 3072
OFF_AK = 4096
OFF_AV = 4352
OFF_GATE = 4608
IN_WIDTH = 7680
GATE_BLK = 1536

LANES = 128
LOG2E = 1.4426950408889634
VMEM_LIMIT = 56 * 1024 * 1024


def _cparams(*sem):
    return pltpu.CompilerParams(dimension_semantics=sem, vmem_limit_bytes=VMEM_LIMIT)


def _tile(n, pref):
    t = min(n, pref)
    while n % t:
        t //= 2
    return t


def _dot(a, b):
    return jnp.dot(a, b, preferred_element_type=F32)


def _dot_nt(a, b):
    return lax.dot_general(a, b, (((1,), (1,)), ((), ())), preferred_element_type=F32)


def _dot_tn(a, b):
    return lax.dot_general(a, b, (((0,), (0,)), ((), ())), preferred_element_type=F32)


def _rms(x, g):
    return x * lax.rsqrt(jnp.mean(x * x, axis=-1, keepdims=True) + EPS) * g


def _norm_matmul_kernel(x_ref, g_ref, w_ref, o_ref, h_ref):
    @pl.when(pl.program_id(1) == 0)
    def _():
        h_ref[...] = _rms(x_ref[...], g_ref[...]).astype(BF16)

    o_ref[...] = _dot(h_ref[...], w_ref[...]).astype(o_ref.dtype)


def _norm_matmul(x, g, w, out_dtype=F32, tm_pref=1024, tn_pref=512):
    t, k = x.shape
    nout = w.shape[1]
    tm = _tile(t, tm_pref)
    tn = _tile(nout, tn_pref)
    return pl.pallas_call(
        _norm_matmul_kernel,
        grid=(t // tm, nout // tn),
        in_specs=[
            pl.BlockSpec((tm, k), lambda i, j: (i, 0)),
            pl.BlockSpec((1, k), lambda i, j: (0, 0)),
            pl.BlockSpec((k, tn), lambda i, j: (0, j)),
        ],
        out_specs=pl.BlockSpec((tm, tn), lambda i, j: (i, j)),
        out_shape=jax.ShapeDtypeStruct((t, nout), out_dtype),
        scratch_shapes=[pltpu.VMEM((tm, k), BF16)],
        compiler_params=_cparams("parallel", "arbitrary"),
        name="norm_in_proj",
    )(x, g, w)


def _pool_kernel(cur_ref, prev_ref, next_ref, w_ref, scale_ref, o_ref, *, n, tn):
    i = pl.program_id(1)
    t0 = i * tn
    u = cur_ref[0]
    prev = jnp.where(i > 0, prev_ref[0], 0.0)
    nxt = jnp.where(t0 + tn < n, next_ref[0], 0.0)
    e = jnp.concatenate([prev, u, nxt], axis=0)
    s2 = e[0:-1] + e[1:]
    c1 = POOL_GDIM
    s4 = s2[0:-2, c1:] + s2[2:, c1:]
    s8 = s4[0:-4, c1:] + s4[4:, c1:]
    s16 = s8[0:-8, c1:] + s8[8:, c1:]
    sums = (s2[7:7 + tn, :c1], s4[6:6 + tn, :c1], s8[4:4 + tn, :c1], s16[0:tn, :])
    t = t0 + lax.broadcasted_iota(jnp.int32, (tn, 1), 0)
    outs = []
    for g in range(POOL_GROUPS):
        h = POOL_HALF[g]
        cnt = (jnp.minimum(t + h, n) - jnp.maximum(t - h, 0)).astype(F32)
        ug = u[:, g * c1:(g + 1) * c1]
        mixed = (sums[g] / cnt - ug).astype(BF16)
        outs.append(_dot(mixed, w_ref[g]))
    y = jnp.concatenate(outs, axis=1) * scale_ref[...]
    o_ref[0] = y.astype(o_ref.dtype)


def _pool_mixer(z, pool_w, pool_scale, tn_pref=512):
    b, n, _ = z.shape
    tn = _tile(n, tn_pref)
    hb = tn // POOL_HALO
    nhb = n // POOL_HALO
    return pl.pallas_call(
        functools.partial(_pool_kernel, n=n, tn=tn),
        grid=(b, n // tn),
        in_specs=[
            pl.BlockSpec((1, tn, POOL_WIDTH), lambda bi, i: (bi, i, OFF_POOL // POOL_WIDTH)),
            pl.BlockSpec((1, POOL_HALO, POOL_WIDTH),
                         lambda bi, i: (bi, jnp.maximum(i * hb - 1, 0), OFF_POOL // POOL_WIDTH)),
            pl.BlockSpec((1, POOL_HALO, POOL_WIDTH),
                         lambda bi, i: (bi, jnp.minimum((i + 1) * hb, nhb - 1), OFF_POOL // POOL_WIDTH)),
            pl.BlockSpec((POOL_GROUPS, POOL_GDIM, POOL_GDIM), lambda bi, i: (0, 0, 0)),
            pl.BlockSpec((1, POOL_WIDTH), lambda bi, i: (0, 0)),
        ],
        out_specs=pl.BlockSpec((1, tn, POOL_WIDTH), lambda bi, i: (bi, i, 0)),
        out_shape=jax.ShapeDtypeStruct((b, n, POOL_WIDTH), BF16),
        compiler_params=_cparams("parallel", "parallel"),
        name="pool_mixer",
    )(z, z, z, pool_w, pool_scale)


def _split3(x):
    hi = x.astype(BF16)
    r1 = x - hi.astype(F32)
    mid = r1.astype(BF16)
    lo = (r1 - mid.astype(F32)).astype(BF16)
    return hi, mid, lo


def _hgrn_chunk(zq, zf, v, log_lb, log1m_lb, one_m_lb, st_ref, rev):
    c = HG_CHUNK
    q = zq * (HG_DK ** -0.5)
    ez = jnp.exp(-jnp.abs(zf))
    log_sig = jnp.minimum(zf, 0.0) - jnp.log1p(ez)
    k = one_m_lb * (jnp.where(zf >= 0.0, ez, 1.0) / (1.0 + ez))
    ga = log_lb
    gb = log1m_lb + log_sig
    logf = jnp.maximum(ga, gb) + jnp.log1p(jnp.exp(-jnp.abs(ga - gb)))

    row = lax.broadcasted_iota(jnp.int32, (c, c), 0)
    col = lax.broadcasted_iota(jnp.int32, (c, c), 1)
    keep = (col >= row) if rev else (col <= row)
    tri = jnp.where(keep, 1.0, 0.0).astype(BF16)
    hi, mid, lo = _split3(logf)
    cum = _dot(tri, hi) + _dot(tri, mid) + _dot(tri, lo)
    last = cum[0:1] if rev else cum[c - 1:c]
    qhat = (q * jnp.exp(cum)).astype(BF16)
    khat = (k * jnp.exp(last - cum)).astype(BF16)
    dec = jnp.exp(last)

    nsub = c // HG_SUB
    a_rows = [[] for _ in range(HG_HEADS)]
    col_s = lax.broadcasted_iota(jnp.int32, (HG_SUB, c), 1)
    for jb in range(nsub):
        r0 = jb * HG_SUB
        qs = q[r0:r0 + HG_SUB]
        cs = cum[r0:r0 + HG_SUB]
        has_off = (jb < nsub - 1) if rev else (jb > 0)
        if has_off:
            ridx = r0 + HG_SUB if rev else r0 - 1
            ref_row = cum[ridx:ridx + 1]
            qt = (qs * jnp.exp(cs - ref_row)).astype(BF16)
            kt = (k * jnp.exp(jnp.minimum(ref_row - cum, 0.0))).astype(BF16)
        blocks = []
        for h in range(HG_HEADS):
            sl = slice(h * HG_DK, (h + 1) * HG_DK)
            if has_off:
                blocks.append(_dot_nt(qt[:, sl], kt[:, sl]))
            else:
                blocks.append(jnp.zeros((HG_SUB, c), F32))
        for l in range(HG_SUB):
            lr = r0 + l
            x = qs * jnp.exp(jnp.minimum(cs - cum[lr:lr + 1], 0.0)) * k[lr:lr + 1]
            for h in range(HG_HEADS):
                s = jnp.sum(x[:, h * HG_DK:(h + 1) * HG_DK], axis=-1, keepdims=True)
                blocks[h] = jnp.where(col_s == lr, s, blocks[h])
        for h in range(HG_HEADS):
            a_rows[h].append(blocks[h])

    outs = []
    for h in range(HG_HEADS):
        sl = slice(h * HG_DK, (h + 1) * HG_DK)
        a = jnp.where(keep, jnp.concatenate(a_rows[h], axis=0), 0.0).astype(BF16)
        vh = v[:, sl].astype(BF16)
        st = st_ref[h]
        outs.append(_dot(a, vh) + _dot_nt(qhat[:, sl], st.astype(BF16)))
        st_ref[h] = st * dec[:, sl] + _dot_tn(vh, khat[:, sl])
    return jnp.concatenate(outs, axis=1)


def _hgrn_fw_kernel(zq_ref, zf_ref, zi_ref, llb_ref, l1m_ref, oml_ref, o_ref, st_ref):
    @pl.when(pl.program_id(1) == 0)
    def _():
        st_ref[...] = jnp.zeros_like(st_ref)

    o_ref[0] = _hgrn_chunk(zq_ref[0], zf_ref[0], zi_ref[0], llb_ref[...], l1m_ref[...], oml_ref[...],
                           st_ref, rev=False)


def _hgrn_bw_kernel(zq_ref, zf_ref, zi_ref, zg_ref, ofw_ref, llb_ref, l1m_ref, oml_ref, gon_ref, o_ref, st_ref):
    @pl.when(pl.program_id(1) == 0)
    def _():
        st_ref[...] = jnp.zeros_like(st_ref)

    o = ofw_ref[0] + _hgrn_chunk(zq_ref[0], zf_ref[0], zi_ref[0], llb_ref[...], l1m_ref[...], oml_ref[...],
                                 st_ref, rev=True)
    zg = zg_ref[0]
    gon = gon_ref[...]
    outs = []
    for h in range(HG_HEADS):
        sl = slice(h * HG_DV, (h + 1) * HG_DV)
        outs.append(_rms(o[:, sl], gon))
    y = jnp.concatenate(outs, axis=1) * (zg * jax.nn.sigmoid(zg))
    o_ref[0] = y.astype(o_ref.dtype)


def _hgrn2_mixer(z, lb_f, lb_b, g_onorm):
    b, n, _ = z.shape
    c = HG_CHUNK
    nc = n // c
    wblk = HG_W

    def col(off):
        return off // wblk

    def rows(lbv):
        lbv = lbv.reshape(1, HG_W).astype(F32)
        return jnp.log(lbv), jnp.log1p(-lbv), 1.0 - lbv

    def zspec(off, rev):
        if rev:
            return pl.BlockSpec((1, c, wblk), lambda bi, i: (bi, nc - 1 - i, col(off)))
        return pl.BlockSpec((1, c, wblk), lambda bi, i: (bi, i, col(off)))

    vec = pl.BlockSpec((1, HG_W), lambda bi, i: (0, 0))
    o_fw = pl.pallas_call(
        _hgrn_fw_kernel,
        grid=(b, nc),
        in_specs=[zspec(OFF_HQ, False), zspec(OFF_HFF, False), zspec(OFF_HI, False), vec, vec, vec],
        out_specs=pl.BlockSpec((1, c, HG_W), lambda bi, i: (bi, i, 0)),
        out_shape=jax.ShapeDtypeStruct((b, n, HG_W), F32),
        scratch_shapes=[pltpu.VMEM((HG_HEADS, HG_DV, HG_DK), F32)],
        compiler_params=_cparams("parallel", "arbitrary"),
        name="hgrn2_forward",
    )(z, z, z, *rows(lb_f))
    return pl.pallas_call(
        _hgrn_bw_kernel,
        grid=(b, nc),
        in_specs=[zspec(OFF_HQ, True), zspec(OFF_HFB, True), zspec(OFF_HI, True), zspec(OFF_HG, True),
                  pl.BlockSpec((1, c, HG_W), lambda bi, i: (bi, nc - 1 - i, 0)), vec, vec, vec,
                  pl.BlockSpec((1, HG_DV), lambda bi, i: (0, 0))],
        out_specs=pl.BlockSpec((1, c, HG_W), lambda bi, i: (bi, nc - 1 - i, 0)),
        out_shape=jax.ShapeDtypeStruct((b, n, HG_W), BF16),
        scratch_shapes=[pltpu.VMEM((HG_HEADS, HG_DV, HG_DK), F32)],
        compiler_params=_cparams("parallel", "arbitrary"),
        name="hgrn2_backward_combine",
    )(z, z, z, z, o_fw, *rows(lb_b), g_onorm.reshape(1, HG_DV).astype(F32))


def _headnorm_rope(x, avg, g, cosv, sinv):
    hi = (x * x).astype(BF16)
    lo = (x * x - hi.astype(F32)).astype(BF16)
    ms = _dot(hi, avg) + _dot(lo, avg)
    y = x * lax.rsqrt(ms + EPS) * g
    lane = lax.broadcasted_iota(jnp.int32, y.shape, 1)
    first = (lane % ROPE_HALF) < (ROPE_HALF // 2)
    partner = jnp.where(first, pltpu.roll(y, LANES - ROPE_HALF // 2, 1), pltpu.roll(y, ROPE_HALF // 2, 1))
    return y * cosv + partner * sinv


def _attn_prep_kernel(zq_ref, zk_ref, zv_ref, cos_ref, sin_ref, gq_ref, gk_ref, avg_ref, qt_ref, k_ref, vt_ref):
    cosv = cos_ref[...]
    sinv = sin_ref[...]
    avg = avg_ref[...]
    for p in range(ATT_QW // LANES):
        sl = slice(p * LANES, (p + 1) * LANES)
        y = _headnorm_rope(zq_ref[0, :, sl], avg, gq_ref[...], cosv, sinv) * (HEAD_DIM ** -0.5 * LOG2E)
        qt_ref[0, sl, :] = y.T.astype(qt_ref.dtype)
    for p in range(ATT_KVW // LANES):
        sl = slice(p * LANES, (p + 1) * LANES)
        k_ref[0, :, sl] = _headnorm_rope(zk_ref[0, :, sl], avg, gk_ref[...], cosv, sinv).astype(k_ref.dtype)
        vt_ref[0, sl, :] = zv_ref[0, :, sl].T.astype(vt_ref.dtype)


def _flash_kernel(qt_ref, k_ref, vt_ref, o_ref, qpad_ref, m_ref, l_ref, acc_ref):
    kv = pl.program_id(2)
    tq = qt_ref.shape[2]
    hd = HEAD_DIM
    group = ATT_HEADS // ATT_KV_HEADS

    @pl.when(kv == 0)
    def _():
        zeros = jnp.zeros((hd, tq), qpad_ref.dtype)
        for h in range(ATT_HEADS):
            qh = qt_ref[0, h * hd:(h + 1) * hd, :]
            if (h // group) % 2 == 0:
                qpad_ref[h] = jnp.concatenate([qh, zeros], axis=0)
            else:
                qpad_ref[h] = jnp.concatenate([zeros, qh], axis=0)
        m_ref[...] = jnp.full_like(m_ref, -jnp.inf)
        l_ref[...] = jnp.zeros_like(l_ref)
        acc_ref[...] = jnp.zeros_like(acc_ref)

    def scores(h):
        a = h // (2 * group)
        return _dot(k_ref[0, :, a * LANES:(a + 1) * LANES], qpad_ref[h])

    st_next = scores(0)
    for h in range(ATT_HEADS):
        st = st_next
        if h + 1 < ATT_HEADS:
            st_next = scores(h + 1)
        g = h // group
        vt = vt_ref[0, g * hd:(g + 1) * hd, :]
        m_prev = m_ref[h]
        m_new = jnp.maximum(m_prev, jnp.max(st, axis=0, keepdims=True))
        alpha = jnp.exp2(m_prev - m_new)
        pt = jnp.exp2(st - m_new)
        l_ref[h] = alpha * l_ref[h] + jnp.sum(pt, axis=0, keepdims=True)
        m_ref[h] = m_new
        rows = slice(h * hd, (h + 1) * hd)
        acc_ref[rows] = acc_ref[rows] * alpha + _dot(vt, pt.astype(BF16))

    @pl.when(kv == pl.num_programs(2) - 1)
    def _():
        for h in range(ATT_HEADS):
            rows = slice(h * hd, (h + 1) * hd)
            acc_ref[rows] = acc_ref[rows] * (1.0 / l_ref[h])
        for p in range(ATT_QW // LANES):
            sl = slice(p * LANES, (p + 1) * LANES)
            o_ref[0, :, sl] = acc_ref[sl].T.astype(o_ref.dtype)


def _rope_tables(n):
    rows = n // GRID_W
    row = jnp.repeat(jnp.arange(rows, dtype=F32), GRID_W)
    colp = jnp.tile(jnp.arange(GRID_W, dtype=F32), rows)
    inv_freq = 1.0 / (ROPE_THETA ** (jnp.arange(0, ROPE_HALF, 2, dtype=F32) / ROPE_HALF))
    ang_r = row[:, None] * inv_freq
    ang_c = colp[:, None] * inv_freq
    cos64 = jnp.concatenate([jnp.cos(ang_r), jnp.cos(ang_r), jnp.cos(ang_c), jnp.cos(ang_c)], axis=-1)
    sin64 = jnp.concatenate([-jnp.sin(ang_r), jnp.sin(ang_r), -jnp.sin(ang_c), jnp.sin(ang_c)], axis=-1)
    return jnp.tile(cos64, (1, 2)), jnp.tile(sin64, (1, 2))


def _gqa_axial(z, g_q, g_k, cos_t, sin_t, tm_pref=512, tq_pref=512, tk_pref=1024):
    b, n, _ = z.shape
    tm = _tile(n, tm_pref)
    seg = lax.broadcasted_iota(jnp.int32, (LANES, LANES), 0) // HEAD_DIM
    seg_t = lax.broadcasted_iota(jnp.int32, (LANES, LANES), 1) // HEAD_DIM
    avg = jnp.where(seg == seg_t, 1.0 / HEAD_DIM, 0.0).astype(BF16)
    gq2 = jnp.tile(g_q.reshape(1, HEAD_DIM).astype(F32), (1, 2))
    gk2 = jnp.tile(g_k.reshape(1, HEAD_DIM).astype(F32), (1, 2))
    vec = pl.BlockSpec((1, LANES), lambda bi, i: (0, 0))
    tab = pl.BlockSpec((tm, LANES), lambda bi, i: (i, 0))
    qt, kh, vt = pl.pallas_call(
        _attn_prep_kernel,
        grid=(b, n // tm),
        in_specs=[
            pl.BlockSpec((1, tm, ATT_QW), lambda bi, i: (bi, i, OFF_AQ // ATT_QW)),
            pl.BlockSpec((1, tm, ATT_KVW), lambda bi, i: (bi, i, OFF_AK // ATT_KVW)),
            pl.BlockSpec((1, tm, ATT_KVW), lambda bi, i: (bi, i, OFF_AV // ATT_KVW)),
            tab, tab, vec, vec,
            pl.BlockSpec((LANES, LANES), lambda bi, i: (0, 0)),
        ],
        out_specs=[
            pl.BlockSpec((1, ATT_QW, tm), lambda bi, i: (bi, 0, i)),
            pl.BlockSpec((1, tm, ATT_KVW), lambda bi, i: (bi, i, 0)),
            pl.BlockSpec((1, ATT_KVW, tm), lambda bi, i: (bi, 0, i)),
        ],
        out_shape=[
            jax.ShapeDtypeStruct((b, ATT_QW, n), BF16),
            jax.ShapeDtypeStruct((b, n, ATT_KVW), BF16),
            jax.ShapeDtypeStruct((b, ATT_KVW, n), BF16),
        ],
        compiler_params=_cparams("parallel", "parallel"),
        name="attn_prep",
    )(z, z, z, cos_t, sin_t, gq2, gk2, avg)

    tq = _tile(n, tq_pref)
    tk = _tile(n, tk_pref)
    return pl.pallas_call(
        _flash_kernel,
        grid=(b, n // tq, n // tk),
        in_specs=[
            pl.BlockSpec((1, ATT_QW, tq), lambda bi, i, j: (bi, 0, i)),
            pl.BlockSpec((1, tk, ATT_KVW), lambda bi, i, j: (bi, j, 0)),
            pl.BlockSpec((1, ATT_KVW, tk), lambda bi, i, j: (bi, 0, j)),
        ],
        out_specs=pl.BlockSpec((1, tq, ATT_QW), lambda bi, i, j: (bi, i, 0)),
        out_shape=jax.ShapeDtypeStruct((b, n, ATT_QW), BF16),
        scratch_shapes=[
            pltpu.VMEM((ATT_HEADS, LANES, tq), BF16),
            pltpu.VMEM((ATT_HEADS, 1, tq), F32),
            pltpu.VMEM((ATT_HEADS, 1, tq), F32),
            pltpu.VMEM((ATT_QW, tq), F32),
        ],
        compiler_params=_cparams("parallel", "parallel", "arbitrary"),
        name="flash_gqa",
    )(qt, kh, vt)


def _merge_kernel(x_ref, yp_ref, yh_ref, ya_ref, ga_ref, gb_ref, wp_ref, wh_ref, wa_ref, wo_ref, o_ref):
    ga = jax.nn.sigmoid(ga_ref[...])
    gb = jax.nn.sigmoid(gb_ref[...])
    d = D_MODEL
    g_pool = ga[:, :d]
    g_hg = jnp.concatenate([ga[:, d:], gb[:, :2 * d - GATE_BLK]], axis=1)
    g_att = gb[:, 2 * d - GATE_BLK:]
    merged = (g_pool * _dot(yp_ref[...], wp_ref[...])
              + g_hg * _dot(yh_ref[...], wh_ref[...])
              + g_att * _dot(ya_ref[...], wa_ref[...]))
    o_ref[...] = x_ref[...] + _dot(merged.astype(BF16), wo_ref[...])


def _merge(x, y_pool, y_hg, y_att, z2, w_p, w_h, w_a, w_o, tm_pref=512):
    t, d = x.shape
    tm = _tile(t, tm_pref)

    def rowblk(w):
        return pl.BlockSpec((tm, w), lambda i: (i, 0))

    def whole(w):
        return pl.BlockSpec(w.shape, lambda i: (0, 0))

    return pl.pallas_call(
        _merge_kernel,
        grid=(t // tm,),
        in_specs=[
            rowblk(d), rowblk(POOL_WIDTH), rowblk(HG_W), rowblk(ATT_QW),
            pl.BlockSpec((tm, GATE_BLK), lambda i: (i, OFF_GATE // GATE_BLK)),
            pl.BlockSpec((tm, GATE_BLK), lambda i: (i, OFF_GATE // GATE_BLK + 1)),
            whole(w_p), whole(w_h), whole(w_a), whole(w_o),
        ],
        out_specs=rowblk(d),
        out_shape=jax.ShapeDtypeStruct((t, d), F32),
        compiler_params=_cparams("parallel"),
        name="merge_out_proj",
    )(x, y_pool, y_hg, y_att, z2, z2, w_p, w_h, w_a, w_o)


def _gelu_tanh(a):
    return 0.5 * a * (1.0 + jnp.tanh(0.7978845608028654 * (a + 0.044715 * (a * a * a))))


def _ffn_kernel(x_ref, prev_ref, next_ref, g_ref, wa_ref, wg_ref, cwa_ref, cwg_ref, cba_ref, cbg_ref, wd_ref,
                o_ref, h_ref, acc_ref, *, n, tm):
    i = pl.program_id(1)
    j = pl.program_id(2)
    hl = FFN_HALO

    @pl.when(j == 0)
    def _():
        g = g_ref[...]
        h_ref[hl:hl + tm] = _rms(x_ref[0], g).astype(BF16)
        h_ref[0:hl] = jnp.where(i > 0, _rms(prev_ref[0], g), 0.0).astype(BF16)
        h_ref[hl + tm:] = jnp.where((i + 1) * tm < n, _rms(next_ref[0], g), 0.0).astype(BF16)
        acc_ref[...] = jnp.zeros_like(acc_ref)

    h = h_ref[...]

    def conv(w_ref, cw_ref, cb_ref):
        u = _dot(h, w_ref[...])
        cw = cw_ref[...]
        return (u[hl - 1:hl - 1 + tm] * cw[0:1] + u[hl:hl + tm] * cw[1:2] + u[hl + 1:hl + 1 + tm] * cw[2:3]
                + cb_ref[...])

    act = _gelu_tanh(conv(wa_ref, cwa_ref, cba_ref)) * conv(wg_ref, cwg_ref, cbg_ref)
    acc_ref[...] += _dot(act.astype(BF16), wd_ref[...])

    @pl.when(j == pl.num_programs(2) - 1)
    def _():
        o_ref[0] = x_ref[0] + acc_ref[...]


def _ffn(x, g, w_up, conv_w, conv_b, w_down, tm_pref=512, tf_pref=256):
    b, n, d = x.shape
    tm = _tile(n, tm_pref)
    tf = _tile(D_FF, tf_pref)
    nf = D_FF // tf
    hb = tm // FFN_HALO
    nhb = n // FFN_HALO
    return pl.pallas_call(
        functools.partial(_ffn_kernel, n=n, tm=tm),
        grid=(b, n // tm, nf),
        in_specs=[
            pl.BlockSpec((1, tm, d), lambda bi, i, j: (bi, i, 0)),
            pl.BlockSpec((1, FFN_HALO, d), lambda bi, i, j: (bi, jnp.maximum(i * hb - 1, 0), 0)),
            pl.BlockSpec((1, FFN_HALO, d), lambda bi, i, j: (bi, jnp.minimum((i + 1) * hb, nhb - 1), 0)),
            pl.BlockSpec((1, d), lambda bi, i, j: (0, 0)),
            pl.BlockSpec((d, tf), lambda bi, i, j: (0, j)),
            pl.BlockSpec((d, tf), lambda bi, i, j: (0, nf + j)),
            pl.BlockSpec((3, tf), lambda bi, i, j: (0, j)),
            pl.BlockSpec((3, tf), lambda bi, i, j: (0, nf + j)),
            pl.BlockSpec((1, tf), lambda bi, i, j: (0, j)),
            pl.BlockSpec((1, tf), lambda bi, i, j: (0, nf + j)),
            pl.BlockSpec((tf, d), lambda bi, i, j: (j, 0)),
        ],
        out_specs=pl.BlockSpec((1, tm, d), lambda bi, i, j: (bi, i, 0)),
        out_shape=jax.ShapeDtypeStruct((b, n, d), F32),
        scratch_shapes=[pltpu.VMEM((tm + 2 * FFN_HALO, d), BF16), pltpu.VMEM((tm, d), F32)],
        compiler_params=_cparams("parallel", "parallel", "arbitrary"),
        name="conv_glu_ffn",
    )(x, x, x, g, w_up, w_up, conv_w, conv_w, conv_b, conv_b, w_down)


def _ple_kernel(x_ref, p_ref, g_ref, wg_ref, wp_ref, gf_ref, o_ref, *, final):
    x = x_ref[...]
    gate = jax.nn.sigmoid(_dot(_rms(x, g_ref[...]).astype(BF16), wg_ref[...]))
    y = x + gate * _dot(p_ref[...].astype(BF16), wp_ref[...])
    if final:
        y = _rms(y, gf_ref[...])
    o_ref[...] = y


def _ple(x, p, g, w_gate, w_ple, g_final, final, tm_pref=512):
    t, d = x.shape
    tm = _tile(t, tm_pref)
    return pl.pallas_call(
        functools.partial(_ple_kernel, final=final),
        grid=(t // tm,),
        in_specs=[
            pl.BlockSpec((tm, d), lambda i: (i, 0)),
            pl.BlockSpec((tm, PLE_DIM), lambda i: (i, 0)),
            pl.BlockSpec((1, d), lambda i: (0, 0)),
            pl.BlockSpec((d, d), lambda i: (0, 0)),
            pl.BlockSpec((PLE_DIM, d), lambda i: (0, 0)),
            pl.BlockSpec((1, d), lambda i: (0, 0)),
        ],
        out_specs=pl.BlockSpec((tm, d), lambda i: (i, 0)),
        out_shape=jax.ShapeDtypeStruct((t, d), F32),
        compiler_params=_cparams("parallel"),
        name="ple_final_norm" if final else "ple",
    )(x, p, g, w_gate, w_ple, g_final)


def _lower_bound(lb_raw):
    s = jnp.cumsum(jax.nn.softmax(lb_raw.astype(F32), axis=0), axis=0)
    return s - s[0:1]


def _trunk(x, p, wts):
    b, n, d = x.shape
    t = b * n
    depth = wts["w_in"].shape[0]
    cos_t, sin_t = _rope_tables(n)
    x2 = x.reshape(t, d)
    for i in range(depth):
        z2 = _norm_matmul(x2, wts["g_mix"][i], wts["w_in"][i])
        z = z2.reshape(b, n, IN_WIDTH)
        y_pool = _pool_mixer(z, wts["pool_w"][i], wts["pool_scale"][i])
        y_hg = _hgrn2_mixer(z, wts["lb_f"][i], wts["lb_b"][i], wts["hg_onorm"][i])
        y_att = _gqa_axial(z, wts["g_q"][i], wts["g_k"][i], cos_t, sin_t)
        x2 = _merge(x2, y_pool.reshape(t, POOL_WIDTH), y_hg.reshape(t, HG_W), y_att.reshape(t, ATT_QW), z2,
                    wts["w_br_pool"][i], wts["w_br_hg"][i], wts["w_br_att"][i], wts["w_out"][i])
        x2 = _ffn(x2.reshape(b, n, d), wts["g_ffn"][i], wts["w_up"][i], wts["conv_w"][i], wts["conv_b"][i],
                  wts["w_down"][i]).reshape(t, d)
        x2 = _ple(x2, p[i].reshape(t, PLE_DIM), wts["g_ple"][i], wts["w_ple_gate"][i], wts["w_ple"][i],
                  wts["g_final"], final=(i == depth - 1))
    return x2.reshape(b, n, d)


def kernel(x_prompt, x_sample, p_prompt, p_sample, g_mix, w_in, pool_w, pool_scale, lb_raw_f, lb_raw_b, hg_onorm,
           g_q, g_k, w_br_pool, w_br_hg, w_br_att, w_out, g_ffn, w_up, conv_w, conv_b, w_down, g_ple, w_ple_gate,
           w_ple, g_final):
    depth = w_in.shape[0]

    def vec(a):
        return a.astype(F32).reshape(depth, 1, a.shape[-1])

    wts = dict(
        g_mix=vec(g_mix), w_in=w_in.astype(BF16), pool_w=pool_w.astype(BF16), pool_scale=vec(pool_scale),
        lb_f=_lower_bound(lb_raw_f), lb_b=_lower_bound(lb_raw_b), hg_onorm=hg_onorm, g_q=g_q, g_k=g_k,
        w_br_pool=w_br_pool.astype(BF16), w_br_hg=w_br_hg.astype(BF16), w_br_att=w_br_att.astype(BF16),
        w_out=w_out.astype(BF16), g_ffn=vec(g_ffn), w_up=w_up.astype(BF16), conv_w=conv_w.astype(F32),
        conv_b=vec(conv_b), w_down=w_down.astype(BF16), g_ple=vec(g_ple), w_ple_gate=w_ple_gate.astype(BF16),
        w_ple=w_ple.astype(BF16), g_final=g_final.astype(F32).reshape(1, -1),
    )
    return _trunk(x_prompt, p_prompt, wts), _trunk(x_sample, p_sample, wts)
```

```python
import functools

import jax
import jax.numpy as jnp
from jax import lax
from jax.experimental import pallas as pl
from jax.experimental.pallas import tpu as pltpu

F32 = jnp.float32
BF16 = jnp.bfloat16

EPS = 1e-6
D_MODEL = 1024
GRID_W = 64
PLE_DIM = 256
POOL_WIDTH = 512
POOL_GROUPS = 4
POOL_GDIM = POOL_WIDTH // POOL_GROUPS
POOL_HALF = (1, 2, 4, 8)
POOL_HALO = 8
HALO_BLK = 16
HG_HEADS = 4
HG_DK = 128
HG_DV = 128
HG_W = HG_HEADS * HG_DK
HG_CHUNK = 64
HG_SUB = 16
ATT_HEADS = 16
ATT_KV_HEADS = 4
HEAD_DIM = 64
ATT_QW = ATT_HEADS * HEAD_DIM
ATT_KVW = ATT_KV_HEADS * HEAD_DIM
ROPE_THETA = 10000.0
ROPE_HALF = HEAD_DIM // 2
D_FF = 2816
FFN_HALO = 8
FFN_DOWN_GROUP = 4

OFF_POOL = 0
OFF_HQ = 512
OFF_HFF = 1024
OFF_HFB = 1536
OFF_HI = 2048
OFF_HG = 2560
OFF_AQ = 3072
OFF_AK = 4096
OFF_AV = 4352
OFF_GATE = 4608
IN_WIDTH = 7680
GATE_BLK = 1536

LANES = 128
LOG2E = 1.4426950408889634
VMEM_LIMIT = 56 * 1024 * 1024


def _cparams(*sem):
    return pltpu.CompilerParams(dimension_semantics=sem, vmem_limit_bytes=VMEM_LIMIT)


def _tile(n, pref):
    t = min(n, pref)
    while n % t:
        t //= 2
    return t


def _dot(a, b):
    return jnp.dot(a, b, preferred_element_type=F32)


def _dot_nt(a, b):
    return lax.dot_general(a, b, (((1,), (1,)), ((), ())), preferred_element_type=F32)


def _dot_tn(a, b):
    return lax.dot_general(a, b, (((0,), (0,)), ((), ())), preferred_element_type=F32)


def _rms(x, g):
    return x * lax.rsqrt(jnp.mean(x * x, axis=-1, keepdims=True) + EPS) * g


def _norm_matmul_kernel(x_ref, g_ref, w_ref, o_ref, *, tn):
    h = _rms(x_ref[...], g_ref[...]).astype(BF16)
    for j in range(w_ref.shape[1] // tn):
        sl = slice(j * tn, (j + 1) * tn)
        o_ref[:, sl] = _dot(h, w_ref[:, sl]).astype(o_ref.dtype)


def _norm_matmul(x, g, w, out_dtype=BF16, tm_pref=512, tn_pref=512):
    t, k = x.shape
    nout = w.shape[1]
    tm = _tile(t, tm_pref)
    tn = _tile(nout, tn_pref)
    return pl.pallas_call(
        functools.partial(_norm_matmul_kernel, tn=tn),
        grid=(t // tm,),
        in_specs=[
            pl.BlockSpec((tm, k), lambda i: (i, 0)),
            pl.BlockSpec((1, k), lambda i: (0, 0)),
            pl.BlockSpec((k, nout), lambda i: (0, 0)),
        ],
        out_specs=pl.BlockSpec((tm, nout), lambda i: (i, 0)),
        out_shape=jax.ShapeDtypeStruct((t, nout), out_dtype),
        compiler_params=_cparams("parallel"),
        name="norm_in_proj",
    )(x, g, w)


def _pool_kernel(cur_ref, prev_ref, next_ref, w_ref, scale_ref, o_ref, *, n, tn):
    i = pl.program_id(1)
    t0 = i * tn
    u = cur_ref[0].astype(F32)
    hl = POOL_HALO
    prev = jnp.where(i > 0, prev_ref[0, HALO_BLK - hl:].astype(F32), 0.0)
    nxt = jnp.where(t0 + tn < n, next_ref[0, :hl].astype(F32), 0.0)
    e = jnp.concatenate([prev, u, nxt], axis=0)
    s2 = e[0:-1] + e[1:]
    c1 = POOL_GDIM
    s4 = s2[0:-2, c1:] + s2[2:, c1:]
    s8 = s4[0:-4, c1:] + s4[4:, c1:]
    s16 = s8[0:-8, c1:] + s8[8:, c1:]
    sums = (s2[7:7 + tn, :c1], s4[6:6 + tn, :c1], s8[4:4 + tn, :c1], s16[0:tn, :])
    t = t0 + lax.broadcasted_iota(jnp.int32, (tn, 1), 0)
    outs = []
    for g in range(POOL_GROUPS):
        h = POOL_HALF[g]
        cnt = (jnp.minimum(t + h, n) - jnp.maximum(t - h, 0)).astype(F32)
        ug = u[:, g * c1:(g + 1) * c1]
        mixed = (sums[g] / cnt - ug).astype(BF16)
        outs.append(_dot(mixed, w_ref[g]))
    y = jnp.concatenate(outs, axis=1) * scale_ref[...]
    o_ref[0] = y.astype(o_ref.dtype)


def _pool_mixer(z, pool_w, pool_scale, tn_pref=512):
    b, n, _ = z.shape
    tn = _tile(n, tn_pref)
    hb = tn // HALO_BLK
    nhb = n // HALO_BLK
    return pl.pallas_call(
        functools.partial(_pool_kernel, n=n, tn=tn),
        grid=(b, n // tn),
        in_specs=[
            pl.BlockSpec((1, tn, POOL_WIDTH), lambda bi, i: (bi, i, OFF_POOL // POOL_WIDTH)),
            pl.BlockSpec((1, HALO_BLK, POOL_WIDTH),
                         lambda bi, i: (bi, jnp.maximum(i * hb - 1, 0), OFF_POOL // POOL_WIDTH)),
            pl.BlockSpec((1, HALO_BLK, POOL_WIDTH),
                         lambda bi, i: (bi, jnp.minimum((i + 1) * hb, nhb - 1), OFF_POOL // POOL_WIDTH)),
            pl.BlockSpec((POOL_GROUPS, POOL_GDIM, POOL_GDIM), lambda bi, i: (0, 0, 0)),
            pl.BlockSpec((1, POOL_WIDTH), lambda bi, i: (0, 0)),
        ],
        out_specs=pl.BlockSpec((1, tn, POOL_WIDTH), lambda bi, i: (bi, i, 0)),
        out_shape=jax.ShapeDtypeStruct((b, n, POOL_WIDTH), BF16),
        compiler_params=_cparams("parallel", "parallel"),
        name="pool_mixer",
    )(z, z, z, pool_w, pool_scale)


def _split3(x):
    hi = x.astype(BF16)
    r1 = x - hi.astype(F32)
    mid = r1.astype(BF16)
    lo = (r1 - mid.astype(F32)).astype(BF16)
    return hi, mid, lo


def _hgrn_chunk(zq, zf, v, log_lb, log1m_lb, one_m_lb, st_ref, rev):
    c = HG_CHUNK
    zq, zf, v = zq.astype(F32), zf.astype(F32), v.astype(F32)
    q = zq * (HG_DK ** -0.5)
    ez = jnp.exp(-jnp.abs(zf))
    log_sig = jnp.minimum(zf, 0.0) - jnp.log1p(ez)
    k = one_m_lb * (jnp.where(zf >= 0.0, ez, 1.0) / (1.0 + ez))
    ga = log_lb
    gb = log1m_lb + log_sig
    logf = jnp.maximum(ga, gb) + jnp.log1p(jnp.exp(-jnp.abs(ga - gb)))

    row = lax.broadcasted_iota(jnp.int32, (c, c), 0)
    col = lax.broadcasted_iota(jnp.int32, (c, c), 1)
    keep = (col >= row) if rev else (col <= row)
    tri = jnp.where(keep, 1.0, 0.0).astype(BF16)
    hi, mid, lo = _split3(logf)
    cum = _dot(tri, hi) + _dot(tri, mid) + _dot(tri, lo)
    last = cum[0:1] if rev else cum[c - 1:c]
    qhat = (q * jnp.exp(cum)).astype(BF16)
    khat = (k * jnp.exp(last - cum)).astype(BF16)
    dec = jnp.exp(last)

    nsub = c // HG_SUB
    a_rows = [[] for _ in range(HG_HEADS)]
    col_s = lax.broadcasted_iota(jnp.int32, (HG_SUB, c), 1)
    for jb in range(nsub):
        r0 = jb * HG_SUB
        qs = q[r0:r0 + HG_SUB]
        cs = cum[r0:r0 + HG_SUB]
        has_off = (jb < nsub - 1) if rev else (jb > 0)
        if has_off:
            ridx = r0 + HG_SUB if rev else r0 - 1
            ref_row = cum[ridx:ridx + 1]
            qt = (qs * jnp.exp(cs - ref_row)).astype(BF16)
            kt = (k * jnp.exp(jnp.minimum(ref_row - cum, 0.0))).astype(BF16)
        blocks = []
        for h in range(HG_HEADS):
            sl = slice(h * HG_DK, (h + 1) * HG_DK)
            if has_off:
                blocks.append(_dot_nt(qt[:, sl], kt[:, sl]))
            else:
                blocks.append(jnp.zeros((HG_SUB, c), F32))
        for l in range(HG_SUB):
            lr = r0 + l
            x = qs * jnp.exp(jnp.minimum(cs - cum[lr:lr + 1], 0.0)) * k[lr:lr + 1]
            for h in range(HG_HEADS):
                s = jnp.sum(x[:, h * HG_DK:(h + 1) * HG_DK], axis=-1, keepdims=True)
                blocks[h] = jnp.where(col_s == lr, s, blocks[h])
        for h in range(HG_HEADS):
            a_rows[h].append(blocks[h])

    outs = []
    for h in range(HG_HEADS):
        sl = slice(h * HG_DK, (h + 1) * HG_DK)
        a = jnp.where(keep, jnp.concatenate(a_rows[h], axis=0), 0.0).astype(BF16)
        vh = v[:, sl].astype(BF16)
        st = st_ref[h]
        outs.append(_dot(a, vh) + _dot_nt(qhat[:, sl], st.astype(BF16)))
        st_ref[h] = st * dec[:, sl] + _dot_tn(vh, khat[:, sl])
    return jnp.concatenate(outs, axis=1)


def _hgrn_fw_kernel(zq_ref, zf_ref, zi_ref, llb_ref, l1m_ref, oml_ref, o_ref, st_ref):
    @pl.when(pl.program_id(1) == 0)
    def _():
        st_ref[...] = jnp.zeros_like(st_ref)

    o_ref[0] = _hgrn_chunk(zq_ref[0], zf_ref[0], zi_ref[0], llb_ref[...], l1m_ref[...], oml_ref[...],
                           st_ref, rev=False)


def _hgrn_bw_kernel(zq_ref, zf_ref, zi_ref, zg_ref, ofw_ref, llb_ref, l1m_ref, oml_ref, gon_ref, o_ref, st_ref):
    @pl.when(pl.program_id(1) == 0)
    def _():
        st_ref[...] = jnp.zeros_like(st_ref)

    o = ofw_ref[0] + _hgrn_chunk(zq_ref[0], zf_ref[0], zi_ref[0], llb_ref[...], l1m_ref[...], oml_ref[...],
                                 st_ref, rev=True)
    zg = zg_ref[0].astype(F32)
    gon = gon_ref[...]
    outs = []
    for h in range(HG_HEADS):
        sl = slice(h * HG_DV, (h + 1) * HG_DV)
        outs.append(_rms(o[:, sl], gon))
    y = jnp.concatenate(outs, axis=1) * (zg * jax.nn.sigmoid(zg))
    o_ref[0] = y.astype(o_ref.dtype)


def _hgrn2_mixer(z, lb_f, lb_b, g_onorm):
    b, n, _ = z.shape
    c = HG_CHUNK
    nc = n // c
    wblk = HG_W

    def col(off):
        return off // wblk

    def rows(lbv):
        lbv = lbv.reshape(1, HG_W).astype(F32)
        return jnp.log(lbv), jnp.log1p(-lbv), 1.0 - lbv

    def zspec(off, rev):
        if rev:
            return pl.BlockSpec((1, c, wblk), lambda bi, i: (bi, nc - 1 - i, col(off)))
        return pl.BlockSpec((1, c, wblk), lambda bi, i: (bi, i, col(off)))

    vec = pl.BlockSpec((1, HG_W), lambda bi, i: (0, 0))
    o_fw = pl.pallas_call(
        _hgrn_fw_kernel,
        grid=(b, nc),
        in_specs=[zspec(OFF_HQ, False), zspec(OFF_HFF, False), zspec(OFF_HI, False), vec, vec, vec],
        out_specs=pl.BlockSpec((1, c, HG_W), lambda bi, i: (bi, i, 0)),
        out_shape=jax.ShapeDtypeStruct((b, n, HG_W), F32),
        scratch_shapes=[pltpu.VMEM((HG_HEADS, HG_DV, HG_DK), F32)],
        compiler_params=_cparams("parallel", "arbitrary"),
        name="hgrn2_forward",
    )(z, z, z, *rows(lb_f))
    return pl.pallas_call(
        _hgrn_bw_kernel,
        grid=(b, nc),
        in_specs=[zspec(OFF_HQ, True), zspec(OFF_HFB, True), zspec(OFF_HI, True), zspec(OFF_HG, True),
                  pl.BlockSpec((1, c, HG_W), lambda bi, i: (bi, nc - 1 - i, 0)), vec, vec, vec,
                  pl.BlockSpec((1, HG_DV), lambda bi, i: (0, 0))],
        out_specs=pl.BlockSpec((1, c, HG_W), lambda bi, i: (bi, nc - 1 - i, 0)),
        out_shape=jax.ShapeDtypeStruct((b, n, HG_W), BF16),
        scratch_shapes=[pltpu.VMEM((HG_HEADS, HG_DV, HG_DK), F32)],
        compiler_params=_cparams("parallel", "arbitrary"),
        name="hgrn2_backward_combine",
    )(z, z, z, z, o_fw, *rows(lb_b), g_onorm.reshape(1, HG_DV).astype(F32))


def _headnorm_rope(x, avg, g, cosv, sinv):
    hi = (x * x).astype(BF16)
    lo = (x * x - hi.astype(F32)).astype(BF16)
    ms = _dot(hi, avg) + _dot(lo, avg)
    y = x * lax.rsqrt(ms + EPS) * g
    lane = lax.broadcasted_iota(jnp.int32, y.shape, 1)
    first = (lane % ROPE_HALF) < (ROPE_HALF // 2)
    partner = jnp.where(first, pltpu.roll(y, LANES - ROPE_HALF // 2, 1), pltpu.roll(y, ROPE_HALF // 2, 1))
    return y * cosv + partner * sinv


def _attn_prep_kernel(zq_ref, zk_ref, zv_ref, cos_ref, sin_ref, gq_ref, gk_ref, avg_ref, qt_ref, k_ref, vt_ref):
    cosv = cos_ref[...]
    sinv = sin_ref[...]
    avg = avg_ref[...]
    for p in range(ATT_QW // LANES):
        sl = slice(p * LANES, (p + 1) * LANES)
        y = _headnorm_rope(zq_ref[0, :, sl].astype(F32), avg, gq_ref[...], cosv, sinv) * (HEAD_DIM ** -0.5 * LOG2E)
        qt_ref[0, sl, :] = y.T.astype(qt_ref.dtype)
    for p in range(ATT_KVW // LANES):
        sl = slice(p * LANES, (p + 1) * LANES)
        kn = _headnorm_rope(zk_ref[0, :, sl].astype(F32), avg, gk_ref[...], cosv, sinv)
        k_ref[0, :, sl] = kn.astype(k_ref.dtype)
        vt_ref[0, sl, :] = zv_ref[0, :, sl].astype(F32).T.astype(vt_ref.dtype)


def _flash_kernel(qt_ref, k_ref, vt_ref, o_ref, qpad_ref, m_ref, l_ref, acc_ref):
    kv = pl.program_id(2)
    tq = qt_ref.shape[2]
    hd = HEAD_DIM
    group = ATT_HEADS // ATT_KV_HEADS

    @pl.when(kv == 0)
    def _():
        zeros = jnp.zeros((hd, tq), qpad_ref.dtype)
        for h in range(ATT_HEADS):
            qh = qt_ref[0, h * hd:(h + 1) * hd, :]
            if (h // group) % 2 == 0:
                qpad_ref[h] = jnp.concatenate([qh, zeros], axis=0)
            else:
                qpad_ref[h] = jnp.concatenate([zeros, qh], axis=0)
        m_ref[...] = jnp.full_like(m_ref, -jnp.inf)
        l_ref[...] = jnp.zeros_like(l_ref)
        acc_ref[...] = jnp.zeros_like(acc_ref)

    def scores(h):
        a = h // (2 * group)
        return _dot(k_ref[0, :, a * LANES:(a + 1) * LANES], qpad_ref[h])

    st_next = scores(0)
    for h in range(ATT_HEADS):
        st = st_next
        if h + 1 < ATT_HEADS:
            st_next = scores(h + 1)
        g = h // group
        vt = vt_ref[0, g * hd:(g + 1) * hd, :]
        m_prev = m_ref[h]
        m_new = jnp.maximum(m_prev, jnp.max(st, axis=0, keepdims=True))
        alpha = jnp.exp2(m_prev - m_new)
        pt = jnp.exp2(st - m_new)
        l_ref[h] = alpha * l_ref[h] + jnp.sum(pt, axis=0, keepdims=True)
        m_ref[h] = m_new
        rows = slice(h * hd, (h + 1) * hd)
        acc_ref[rows] = acc_ref[rows] * alpha + _dot(vt, pt.astype(BF16))

    @pl.when(kv == pl.num_programs(2) - 1)
    def _():
        for h in range(ATT_HEADS):
            rows = slice(h * hd, (h + 1) * hd)
            acc_ref[rows] = acc_ref[rows] * (1.0 / l_ref[h])
        for p in range(ATT_QW // LANES):
            sl = slice(p * LANES, (p + 1) * LANES)
            o_ref[0, :, sl] = acc_ref[sl].T.astype(o_ref.dtype)


def _rope_tables(n):
    rows = n // GRID_W
    row = jnp.repeat(jnp.arange(rows, dtype=F32), GRID_W)
    colp = jnp.tile(jnp.arange(GRID_W, dtype=F32), rows)
    inv_freq = 1.0 / (ROPE_THETA ** (jnp.arange(0, ROPE_HALF, 2, dtype=F32) / ROPE_HALF))
    ang_r = row[:, None] * inv_freq
    ang_c = colp[:, None] * inv_freq
    cos64 = jnp.concatenate([jnp.cos(ang_r), jnp.cos(ang_r), jnp.cos(ang_c), jnp.cos(ang_c)], axis=-1)
    sin64 = jnp.concatenate([-jnp.sin(ang_r), jnp.sin(ang_r), -jnp.sin(ang_c), jnp.sin(ang_c)], axis=-1)
    return jnp.tile(cos64, (1, 2)), jnp.tile(sin64, (1, 2))


def _gqa_axial(z, g_q, g_k, cos_t, sin_t, tm_pref=512, tq_pref=512, tk_pref=1024):
    b, n, _ = z.shape
    tm = _tile(n, tm_pref)
    seg = lax.broadcasted_iota(jnp.int32, (LANES, LANES), 0) // HEAD_DIM
    seg_t = lax.broadcasted_iota(jnp.int32, (LANES, LANES), 1) // HEAD_DIM
    avg = jnp.where(seg == seg_t, 1.0 / HEAD_DIM, 0.0).astype(BF16)
    gq2 = jnp.tile(g_q.reshape(1, HEAD_DIM).astype(F32), (1, 2))
    gk2 = jnp.tile(g_k.reshape(1, HEAD_DIM).astype(F32), (1, 2))
    vec = pl.BlockSpec((1, LANES), lambda bi, i: (0, 0))
    tab = pl.BlockSpec((tm, LANES), lambda bi, i: (i, 0))
    qt, kh, vt = pl.pallas_call(
        _attn_prep_kernel,
        grid=(b, n // tm),
        in_specs=[
            pl.BlockSpec((1, tm, ATT_QW), lambda bi, i: (bi, i, OFF_AQ // ATT_QW)),
            pl.BlockSpec((1, tm, ATT_KVW), lambda bi, i: (bi, i, OFF_AK // ATT_KVW)),
            pl.BlockSpec((1, tm, ATT_KVW), lambda bi, i: (bi, i, OFF_AV // ATT_KVW)),
            tab, tab, vec, vec,
            pl.BlockSpec((LANES, LANES), lambda bi, i: (0, 0)),
        ],
        out_specs=[
            pl.BlockSpec((1, ATT_QW, tm), lambda bi, i: (bi, 0, i)),
            pl.BlockSpec((1, tm, ATT_KVW), lambda bi, i: (bi, i, 0)),
            pl.BlockSpec((1, ATT_KVW, tm), lambda bi, i: (bi, 0, i)),
        ],
        out_shape=[
            jax.ShapeDtypeStruct((b, ATT_QW, n), BF16),
            jax.ShapeDtypeStruct((b, n, ATT_KVW), BF16),
            jax.ShapeDtypeStruct((b, ATT_KVW, n), BF16),
        ],
        compiler_params=_cparams("parallel", "parallel"),
        name="attn_prep",
    )(z, z, z, cos_t, sin_t, gq2, gk2, avg)

    tq = _tile(n, tq_pref)
    tk = _tile(n, tk_pref)
    return pl.pallas_call(
        _flash_kernel,
        grid=(b, n // tq, n // tk),
        in_specs=[
            pl.BlockSpec((1, ATT_QW, tq), lambda bi, i, j: (bi, 0, i)),
            pl.BlockSpec((1, tk, ATT_KVW), lambda bi, i, j: (bi, j, 0)),
            pl.BlockSpec((1, ATT_KVW, tk), lambda bi, i, j: (bi, 0, j)),
        ],
        out_specs=pl.BlockSpec((1, tq, ATT_QW), lambda bi, i, j: (bi, i, 0)),
        out_shape=jax.ShapeDtypeStruct((b, n, ATT_QW), BF16),
        scratch_shapes=[
            pltpu.VMEM((ATT_HEADS, LANES, tq), BF16),
            pltpu.VMEM((ATT_HEADS, 1, tq), F32),
            pltpu.VMEM((ATT_HEADS, 1, tq), F32),
            pltpu.VMEM((ATT_QW, tq), F32),
        ],
        compiler_params=_cparams("parallel", "parallel", "arbitrary"),
        name="flash_gqa",
    )(qt, kh, vt)


def _merge_kernel(x_ref, yp_ref, yh_ref, ya_ref, ga_ref, gb_ref, wp_ref, wh_ref, wa_ref, wo_ref, o_ref):
    ga = jax.nn.sigmoid(ga_ref[...].astype(F32))
    gb = jax.nn.sigmoid(gb_ref[...].astype(F32))
    d = D_MODEL
    g_pool = ga[:, :d]
    g_hg = jnp.concatenate([ga[:, d:], gb[:, :2 * d - GATE_BLK]], axis=1)
    g_att = gb[:, 2 * d - GATE_BLK:]
    merged = (g_pool * _dot(yp_ref[...], wp_ref[...])
              + g_hg * _dot(yh_ref[...], wh_ref[...])
              + g_att * _dot(ya_ref[...], wa_ref[...]))
    o_ref[...] = x_ref[...] + _dot(merged.astype(BF16), wo_ref[...])


def _merge(x, y_pool, y_hg, y_att, z2, w_p, w_h, w_a, w_o, tm_pref=512):
    t, d = x.shape
    tm = _tile(t, tm_pref)

    def rowblk(w):
        return pl.BlockSpec((tm, w), lambda i: (i, 0))

    def whole(w):
        return pl.BlockSpec(w.shape, lambda i: (0, 0))

    return pl.pallas_call(
        _merge_kernel,
        grid=(t // tm,),
        in_specs=[
            rowblk(d), rowblk(POOL_WIDTH), rowblk(HG_W), rowblk(ATT_QW),
            pl.BlockSpec((tm, GATE_BLK), lambda i: (i, OFF_GATE // GATE_BLK)),
            pl.BlockSpec((tm, GATE_BLK), lambda i: (i, OFF_GATE // GATE_BLK + 1)),
            whole(w_p), whole(w_h), whole(w_a), whole(w_o),
        ],
        out_specs=rowblk(d),
        out_shape=jax.ShapeDtypeStruct((t, d), F32),
        compiler_params=_cparams("parallel"),
        name="merge_out_proj",
    )(x, y_pool, y_hg, y_att, z2, z2, w_p, w_h, w_a, w_o)


def _gelu_tanh(a):
    half = 0.5 * a
    inner = a * (0.7978845608028654 + (0.7978845608028654 * 0.044715) * (a * a))
    return half + half * jnp.tanh(inner)


def _ffn_kernel(x_ref, prev_ref, next_ref, g_ref, wu_ref, cw_ref, cb_ref, wd_ref, o_ref, h_ref, *, n, tm, tf):
    i = pl.program_id(1)
    hl = FFN_HALO
    g = g_ref[...]
    h_ref[hl:hl + tm] = _rms(x_ref[0], g).astype(BF16)
    h_ref[0:hl] = jnp.where(i > 0, _rms(prev_ref[0], g), 0.0).astype(BF16)
    h_ref[hl + tm:] = jnp.where((i + 1) * tm < n, _rms(next_ref[0], g), 0.0).astype(BF16)
    h = h_ref[...]
    nf = D_FF // tf

    def up(j):
        return tuple(_dot(h, wu_ref[:, c0 + j * tf:c0 + (j + 1) * tf]) for c0 in (0, D_FF))

    rows = tm + 2 * hl

    def conv(u, c0):
        cw = cw_ref[:, c0:c0 + tf]
        prev = pltpu.roll(u, 1, 0)[hl:hl + tm]
        nxt_ = pltpu.roll(u, rows - 1, 0)[hl:hl + tm]
        return prev * cw[0:1] + u[hl:hl + tm] * cw[1:2] + nxt_ * cw[2:3] + cb_ref[:, c0:c0 + tf]

    acc = None
    acts = []
    nxt = up(0)
    for j in range(nf):
        ua, ug = nxt
        if j + 1 < nf:
            nxt = up(j + 1)
        acts.append((_gelu_tanh(conv(ua, j * tf)) * conv(ug, D_FF + j * tf)).astype(BF16))
        if len(acts) == FFN_DOWN_GROUP or j == nf - 1:
            k0 = (j + 1 - len(acts)) * tf
            part = _dot(jnp.concatenate(acts, axis=1), wd_ref[k0:(j + 1) * tf, :])
            acc = part if acc is None else acc + part
            acts = []
    o_ref[0] = x_ref[0] + acc


def _ffn(x, g, w_up, conv_w, conv_b, w_down, tm_pref=512, tf_pref=256):
    b, n, d = x.shape
    tm = _tile(n, tm_pref)
    tf = _tile(D_FF, tf_pref)
    hb = tm // FFN_HALO
    nhb = n // FFN_HALO

    def whole(a):
        return pl.BlockSpec(a.shape, lambda bi, i: (0,) * a.ndim)

    return pl.pallas_call(
        functools.partial(_ffn_kernel, n=n, tm=tm, tf=tf),
        grid=(b, n // tm),
        in_specs=[
            pl.BlockSpec((1, tm, d), lambda bi, i: (bi, i, 0)),
            pl.BlockSpec((1, FFN_HALO, d), lambda bi, i: (bi, jnp.maximum(i * hb - 1, 0), 0)),
            pl.BlockSpec((1, FFN_HALO, d), lambda bi, i: (bi, jnp.minimum((i + 1) * hb, nhb - 1), 0)),
            whole(g), whole(w_up), whole(conv_w), whole(conv_b), whole(w_down),
        ],
        out_specs=pl.BlockSpec((1, tm, d), lambda bi, i: (bi, i, 0)),
        out_shape=jax.ShapeDtypeStruct((b, n, d), F32),
        scratch_shapes=[pltpu.VMEM((tm + 2 * FFN_HALO, d), BF16)],
        compiler_params=_cparams("parallel", "parallel"),
        name="conv_glu_ffn",
    )(x, x, x, g, w_up, conv_w, conv_b, w_down)


def _ple_kernel(x_ref, p_ref, g_ref, wg_ref, wp_ref, gf_ref, o_ref, *, final):
    x = x_ref[...]
    gate = jax.nn.sigmoid(_dot(_rms(x, g_ref[...]).astype(BF16), wg_ref[...]))
    y = x + gate * _dot(p_ref[...].astype(BF16), wp_ref[...])
    if final:
        y = _rms(y, gf_ref[...])
    o_ref[...] = y


def _ple(x, p, g, w_gate, w_ple, g_final, final, tm_pref=512):
    t, d = x.shape
    tm = _tile(t, tm_pref)
    return pl.pallas_call(
        functools.partial(_ple_kernel, final=final),
        grid=(t // tm,),
        in_specs=[
            pl.BlockSpec((tm, d), lambda i: (i, 0)),
            pl.BlockSpec((tm, PLE_DIM), lambda i: (i, 0)),
            pl.BlockSpec((1, d), lambda i: (0, 0)),
            pl.BlockSpec((d, d), lambda i: (0, 0)),
            pl.BlockSpec((PLE_DIM, d), lambda i: (0, 0)),
            pl.BlockSpec((1, d), lambda i: (0, 0)),
        ],
        out_specs=pl.BlockSpec((tm, d), lambda i: (i, 0)),
        out_shape=jax.ShapeDtypeStruct((t, d), F32),
        compiler_params=_cparams("parallel"),
        name="ple_final_norm" if final else "ple",
    )(x, p, g, w_gate, w_ple, g_final)


def _lower_bound(lb_raw):
    s = jnp.cumsum(jax.nn.softmax(lb_raw.astype(F32), axis=0), axis=0)
    return s - s[0:1]


def _trunk(x, p, wts):
    b, n, d = x.shape
    t = b * n
    depth = wts["w_in"].shape[0]
    cos_t, sin_t = _rope_tables(n)
    x2 = x.reshape(t, d)
    for i in range(depth):
        z2 = _norm_matmul(x2, wts["g_mix"][i], wts["w_in"][i])
        z = z2.reshape(b, n, IN_WIDTH)
        y_pool = _pool_mixer(z, wts["pool_w"][i], wts["pool_scale"][i])
        y_hg = _hgrn2_mixer(z, wts["lb_f"][i], wts["lb_b"][i], wts["hg_onorm"][i])
        y_att = _gqa_axial(z, wts["g_q"][i], wts["g_k"][i], cos_t, sin_t)
        x2 = _merge(x2, y_pool.reshape(t, POOL_WIDTH), y_hg.reshape(t, HG_W), y_att.reshape(t, ATT_QW), z2,
                    wts["w_br_pool"][i], wts["w_br_hg"][i], wts["w_br_att"][i], wts["w_out"][i])
        x2 = _ffn(x2.reshape(b, n, d), wts["g_ffn"][i], wts["w_up"][i], wts["conv_w"][i], wts["conv_b"][i],
                  wts["w_down"][i]).reshape(t, d)
        x2 = _ple(x2, p[i].reshape(t, PLE_DIM), wts["g_ple"][i], wts["w_ple_gate"][i], wts["w_ple"][i],
                  wts["g_final"], final=(i == depth - 1))
    return x2.reshape(b, n, d)


def kernel(x_prompt, x_sample, p_prompt, p_sample, g_mix, w_in, pool_w, pool_scale, lb_raw_f, lb_raw_b, hg_onorm,
           g_q, g_k, w_br_pool, w_br_hg, w_br_att, w_out, g_ffn, w_up, conv_w, conv_b, w_down, g_ple, w_ple_gate,
           w_ple, g_final):
    depth = w_in.shape[0]

    def vec(a):
        return a.astype(F32).reshape(depth, 1, a.shape[-1])

    wts = dict(
        g_mix=vec(g_mix), w_in=w_in.astype(BF16), pool_w=pool_w.astype(BF16), pool_scale=vec(pool_scale),
        lb_f=_lower_bound(lb_raw_f), lb_b=_lower_bound(lb_raw_b), hg_onorm=hg_onorm, g_q=g_q, g_k=g_k,
        w_br_pool=w_br_pool.astype(BF16), w_br_hg=w_br_hg.astype(BF16), w_br_att=w_br_att.astype(BF16),
        w_out=w_out.astype(BF16), g_ffn=vec(g_ffn), w_up=w_up.astype(BF16), conv_w=conv_w.astype(F32),
        conv_b=vec(conv_b), w_down=w_down.astype(BF16), g_ple=vec(g_ple), w_ple_gate=w_ple_gate.astype(BF16),
        w_ple=w_ple.astype(BF16), g_final=g_final.astype(F32).reshape(1, -1),
    )
    return _trunk(x_prompt, p_prompt, wts), _trunk(x_sample, p_sample, wts)
```

```python
import functools

import jax
import jax.numpy as jnp
from jax import lax
from jax.experimental import pallas as pl
from jax.experimental.pallas import tpu as pltpu

F32 = jnp.float32
BF16 = jnp.bfloat16

EPS = 1e-6
D_MODEL = 1024
GRID_W = 64
PLE_DIM = 256
POOL_WIDTH = 512
POOL_GROUPS = 4
POOL_GDIM = POOL_WIDTH // POOL_GROUPS
POOL_HALF = (1, 2, 4, 8)
POOL_HALO = 8
HALO_BLK = 16
HG_HEADS = 4
HG_DK = 128
HG_DV = 128
HG_W = HG_HEADS * HG_DK
HG_CHUNK = 64
HG_SUB = 16
ATT_HEADS = 16
ATT_KV_HEADS = 4
HEAD_DIM = 64
ATT_QW = ATT_HEADS * HEAD_DIM
ATT_KVW = ATT_KV_HEADS * HEAD_DIM
ROPE_THETA = 10000.0
ROPE_HALF = HEAD_DIM // 2
D_FF = 2816
FFN_HALO = 8
FFN_DOWN_GROUP = 4

OFF_POOL = 0
OFF_HQ = 512
OFF_HFF = 1024
OFF_HFB = 1536
OFF_HI = 2048
OFF_HG = 2560
OFF_AQ = 3072
OFF_AK = 4096
OFF_AV = 4352
OFF_GATE = 4608
IN_WIDTH = 7680
GATE_BLK = 1536

LANES = 128
LOG2E = 1.4426950408889634
VMEM_LIMIT = 56 * 1024 * 1024


def _cparams(*sem):
    return pltpu.CompilerParams(dimension_semantics=sem, vmem_limit_bytes=VMEM_LIMIT)


def _tile(n, pref):
    t = min(n, pref)
    while n % t:
        t //= 2
    return t


def _dot(a, b):
    return jnp.dot(a, b, preferred_element_type=F32)


def _dot_nt(a, b):
    return lax.dot_general(a, b, (((1,), (1,)), ((), ())), preferred_element_type=F32)


def _dot_tn(a, b):
    return lax.dot_general(a, b, (((0,), (0,)), ((), ())), preferred_element_type=F32)


def _rms(x, g):
    return x * lax.rsqrt(jnp.mean(x * x, axis=-1, keepdims=True) + EPS) * g


def _norm_matmul_kernel(x_ref, g_ref, w_ref, o_ref, *, tn):
    h = _rms(x_ref[...], g_ref[...]).astype(BF16)
    for j in range(w_ref.shape[1] // tn):
        sl = slice(j * tn, (j + 1) * tn)
        o_ref[:, sl] = _dot(h, w_ref[:, sl]).astype(o_ref.dtype)


def _norm_matmul(x, g, w, out_dtype=BF16, tm_pref=512, tn_pref=512):
    t, k = x.shape
    nout = w.shape[1]
    tm = _tile(t, tm_pref)
    tn = _tile(nout, tn_pref)
    return pl.pallas_call(
        functools.partial(_norm_matmul_kernel, tn=tn),
        grid=(t // tm,),
        in_specs=[
            pl.BlockSpec((tm, k), lambda i: (i, 0)),
            pl.BlockSpec((1, k), lambda i: (0, 0)),
            pl.BlockSpec((k, nout), lambda i: (0, 0)),
        ],
        out_specs=pl.BlockSpec((tm, nout), lambda i: (i, 0)),
        out_shape=jax.ShapeDtypeStruct((t, nout), out_dtype),
        compiler_params=_cparams("parallel"),
        name="norm_in_proj",
    )(x, g, w)


def _pool_kernel(cur_ref, prev_ref, next_ref, w_ref, scale_ref, o_ref, *, n, tn):
    i = pl.program_id(1)
    t0 = i * tn
    u = cur_ref[0].astype(F32)
    hl = POOL_HALO
    prev = jnp.where(i > 0, prev_ref[0, HALO_BLK - hl:].astype(F32), 0.0)
    nxt = jnp.where(t0 + tn < n, next_ref[0, :hl].astype(F32), 0.0)
    e = jnp.concatenate([prev, u, nxt], axis=0)
    s2 = e[0:-1] + e[1:]
    c1 = POOL_GDIM
    s4 = s2[0:-2, c1:] + s2[2:, c1:]
    s8 = s4[0:-4, c1:] + s4[4:, c1:]
    s16 = s8[0:-8, c1:] + s8[8:, c1:]
    sums = (s2[7:7 + tn, :c1], s4[6:6 + tn, :c1], s8[4:4 + tn, :c1], s16[0:tn, :])
    t = t0 + lax.broadcasted_iota(jnp.int32, (tn, 1), 0)
    outs = []
    for g in range(POOL_GROUPS):
        h = POOL_HALF[g]
        cnt = (jnp.minimum(t + h, n) - jnp.maximum(t - h, 0)).astype(F32)
        ug = u[:, g * c1:(g + 1) * c1]
        mixed = (sums[g] / cnt - ug).astype(BF16)
        outs.append(_dot(mixed, w_ref[g]))
    y = jnp.concatenate(outs, axis=1) * scale_ref[...]
    o_ref[0] = y.astype(o_ref.dtype)


def _pool_mixer(z, pool_w, pool_scale, tn_pref=512):
    b, n, _ = z.shape
    tn = _tile(n, tn_pref)
    hb = tn // HALO_BLK
    nhb = n // HALO_BLK
    return pl.pallas_call(
        functools.partial(_pool_kernel, n=n, tn=tn),
        grid=(b, n // tn),
        in_specs=[
            pl.BlockSpec((1, tn, POOL_WIDTH), lambda bi, i: (bi, i, OFF_POOL // POOL_WIDTH)),
            pl.BlockSpec((1, HALO_BLK, POOL_WIDTH),
                         lambda bi, i: (bi, jnp.maximum(i * hb - 1, 0), OFF_POOL // POOL_WIDTH)),
            pl.BlockSpec((1, HALO_BLK, POOL_WIDTH),
                         lambda bi, i: (bi, jnp.minimum((i + 1) * hb, nhb - 1), OFF_POOL // POOL_WIDTH)),
            pl.BlockSpec((POOL_GROUPS, POOL_GDIM, POOL_GDIM), lambda bi, i: (0, 0, 0)),
            pl.BlockSpec((1, POOL_WIDTH), lambda bi, i: (0, 0)),
        ],
        out_specs=pl.BlockSpec((1, tn, POOL_WIDTH), lambda bi, i: (bi, i, 0)),
        out_shape=jax.ShapeDtypeStruct((b, n, POOL_WIDTH), BF16),
        compiler_params=_cparams("parallel", "parallel"),
        name="pool_mixer",
    )(z, z, z, pool_w, pool_scale)


def _split3(x):
    hi = x.astype(BF16)
    r1 = x - hi.astype(F32)
    mid = r1.astype(BF16)
    lo = (r1 - mid.astype(F32)).astype(BF16)
    return hi, mid, lo


def _hgrn_chunk(zq, zf, v, log_lb, log1m_lb, one_m_lb, st_ref, rev):
    c = HG_CHUNK
    zq, zf, v = zq.astype(F32), zf.astype(F32), v.astype(F32)
    q = zq * (HG_DK ** -0.5)
    ez = jnp.exp(-jnp.abs(zf))
    log_sig = jnp.minimum(zf, 0.0) - jnp.log1p(ez)
    k = one_m_lb * (jnp.where(zf >= 0.0, ez, 1.0) / (1.0 + ez))
    ga = log_lb
    gb = log1m_lb + log_sig
    logf = jnp.maximum(ga, gb) + jnp.log1p(jnp.exp(-jnp.abs(ga - gb)))

    row = lax.broadcasted_iota(jnp.int32, (c, c), 0)
    col = lax.broadcasted_iota(jnp.int32, (c, c), 1)
    keep = (col >= row) if rev else (col <= row)
    tri = jnp.where(keep, 1.0, 0.0).astype(BF16)
    hi, mid, lo = _split3(logf)
    cum = (_dot(tri, hi) + _dot(tri, mid) + _dot(tri, lo)) * LOG2E
    last = cum[0:1] if rev else cum[c - 1:c]
    qhat = (q * jnp.exp2(cum)).astype(BF16)
    khat = (k * jnp.exp2(last - cum)).astype(BF16)
    dec = jnp.exp2(last)

    nsub = c // HG_SUB
    half = HG_SUB // 2
    heads = [slice(h * HG_DK, (h + 1) * HG_DK) for h in range(HG_HEADS)]
    a_rows = [[] for _ in range(HG_HEADS)]
    col_h = lax.broadcasted_iota(jnp.int32, (half, c), 1)
    for jb in range(nsub):
        r0 = jb * HG_SUB
        lo_r, hi_r = (r0 + HG_SUB, c) if rev else (0, r0)
        if hi_r > lo_r:
            ridx = lo_r if rev else hi_r - 1
            ref_row = cum[ridx:ridx + 1]
            qt = (q[r0:r0 + HG_SUB] * jnp.exp2(cum[r0:r0 + HG_SUB] - ref_row)).astype(BF16)
            pieces = [jnp.zeros((lo_r, HG_W), BF16),
                      (k[lo_r:hi_r] * jnp.exp2(ref_row - cum[lo_r:hi_r])).astype(BF16),
                      jnp.zeros((c - hi_r, HG_W), BF16)]
            kt = jnp.concatenate([p for p in pieces if p.shape[0]], axis=0)
            offs = [_dot_nt(qt[:, sl], kt[:, sl]) for sl in heads]
            halves = [[o[:half] for o in offs], [o[half:] for o in offs]]
        else:
            halves = [[jnp.zeros((half, c), F32) for _ in heads] for _ in range(2)]
        for l in range(HG_SUB):
            lr = r0 + l
            for which in range(2):
                if (which == 0 and not rev and l >= half) or (which == 1 and rev and l < half):
                    continue
                rr = r0 + which * half
                x = q[rr:rr + half] * jnp.exp2(jnp.minimum(cum[rr:rr + half] - cum[lr:lr + 1], 0.0)) * k[lr:lr + 1]
                for h, sl in enumerate(heads):
                    s = jnp.sum(x[:, sl], axis=-1, keepdims=True)
                    halves[which][h] = jnp.where(col_h == lr, s, halves[which][h])
        for h in range(HG_HEADS):
            a_rows[h] += [halves[0][h], halves[1][h]]

    outs = []
    for h, sl in enumerate(heads):
        a = jnp.where(keep, jnp.concatenate(a_rows[h], axis=0), 0.0).astype(BF16)
        vh = v[:, sl].astype(BF16)
        st = st_ref[h]
        outs.append(_dot(a, vh) + _dot_nt(qhat[:, sl], st.astype(BF16)))
        st_ref[h] = st * dec[:, sl] + _dot_tn(vh, khat[:, sl])
    return jnp.concatenate(outs, axis=1)


def _hgrn_kernel(*refs, rev, combine, chunks):
    if combine:
        zq_ref, zf_ref, zi_ref, zg_ref, ofw_ref, llb_ref, l1m_ref, oml_ref, gon_ref, o_ref, st_ref = refs
    else:
        zq_ref, zf_ref, zi_ref, llb_ref, l1m_ref, oml_ref, o_ref, st_ref = refs

    @pl.when(pl.program_id(1) == 0)
    def _():
        st_ref[...] = jnp.zeros_like(st_ref)

    order = range(chunks - 1, -1, -1) if rev else range(chunks)
    for ci in order:
        rows = slice(ci * HG_CHUNK, (ci + 1) * HG_CHUNK)
        o = _hgrn_chunk(zq_ref[0, rows], zf_ref[0, rows], zi_ref[0, rows], llb_ref[...], l1m_ref[...],
                        oml_ref[...], st_ref, rev)
        if combine:
            o = o + ofw_ref[0, rows]
            zg = zg_ref[0, rows].astype(F32)
            y = jnp.concatenate([_rms(o[:, h * HG_DV:(h + 1) * HG_DV], gon_ref[...]) for h in range(HG_HEADS)],
                                axis=1)
            o = y * (zg * jax.nn.sigmoid(zg))
        o_ref[0, rows] = o.astype(o_ref.dtype)


def _hgrn2_mixer(z, lb_f, lb_b, g_onorm, chunks_pref=4):
    b, n, _ = z.shape
    chunks = chunks_pref if (n // HG_CHUNK) % chunks_pref == 0 else 1
    rows_blk = HG_CHUNK * chunks
    nblk = n // rows_blk

    def lb_rows(lbv):
        lbv = lbv.reshape(1, HG_W).astype(F32)
        return jnp.log(lbv), jnp.log1p(-lbv), 1.0 - lbv

    def spec(col, rev):
        if rev:
            return pl.BlockSpec((1, rows_blk, HG_W), lambda bi, i: (bi, nblk - 1 - i, col))
        return pl.BlockSpec((1, rows_blk, HG_W), lambda bi, i: (bi, i, col))

    vec = pl.BlockSpec((1, HG_W), lambda bi, i: (0, 0))
    scratch = [pltpu.VMEM((HG_HEADS, HG_DV, HG_DK), F32)]
    o_fw = pl.pallas_call(
        functools.partial(_hgrn_kernel, rev=False, combine=False, chunks=chunks),
        grid=(b, nblk),
        in_specs=[spec(OFF_HQ // HG_W, False), spec(OFF_HFF // HG_W, False), spec(OFF_HI // HG_W, False),
                  vec, vec, vec],
        out_specs=spec(0, False),
        out_shape=jax.ShapeDtypeStruct((b, n, HG_W), F32),
        scratch_shapes=scratch,
        compiler_params=_cparams("parallel", "arbitrary"),
        name="hgrn2_forward",
    )(z, z, z, *lb_rows(lb_f))
    return pl.pallas_call(
        functools.partial(_hgrn_kernel, rev=True, combine=True, chunks=chunks),
        grid=(b, nblk),
        in_specs=[spec(OFF_HQ // HG_W, True), spec(OFF_HFB // HG_W, True), spec(OFF_HI // HG_W, True),
                  spec(OFF_HG // HG_W, True), spec(0, True), vec, vec, vec,
                  pl.BlockSpec((1, HG_DV), lambda bi, i: (0, 0))],
        out_specs=spec(0, True),
        out_shape=jax.ShapeDtypeStruct((b, n, HG_W), BF16),
        scratch_shapes=scratch,
        compiler_params=_cparams("parallel", "arbitrary"),
        name="hgrn2_backward_combine",
    )(z, z, z, z, o_fw, *lb_rows(lb_b), g_onorm.reshape(1, HG_DV).astype(F32))


def _headnorm_rope(x, avg, g, cosv, sinv):
    hi = (x * x).astype(BF16)
    lo = (x * x - hi.astype(F32)).astype(BF16)
    ms = _dot(hi, avg) + _dot(lo, avg)
    y = x * lax.rsqrt(ms + EPS) * g
    lane = lax.broadcasted_iota(jnp.int32, y.shape, 1)
    first = (lane % ROPE_HALF) < (ROPE_HALF // 2)
    partner = jnp.where(first, pltpu.roll(y, LANES - ROPE_HALF // 2, 1), pltpu.roll(y, ROPE_HALF // 2, 1))
    return y * cosv + partner * sinv


def _attn_prep_kernel(zq_ref, zk_ref, zv_ref, cos_ref, sin_ref, gq_ref, gk_ref, avg_ref, qt_ref, k_ref, vt_ref):
    cosv = cos_ref[...]
    sinv = sin_ref[...]
    avg = avg_ref[...]
    for p in range(ATT_QW // LANES):
        sl = slice(p * LANES, (p + 1) * LANES)
        y = _headnorm_rope(zq_ref[0, :, sl].astype(F32), avg, gq_ref[...], cosv, sinv) * (HEAD_DIM ** -0.5 * LOG2E)
        qt_ref[0, sl, :] = y.T.astype(qt_ref.dtype)
    for p in range(ATT_KVW // LANES):
        sl = slice(p * LANES, (p + 1) * LANES)
        kn = _headnorm_rope(zk_ref[0, :, sl].astype(F32), avg, gk_ref[...], cosv, sinv)
        k_ref[0, :, sl] = kn.astype(k_ref.dtype)
        vt_ref[0, sl, :] = zv_ref[0, :, sl].astype(F32).T.astype(vt_ref.dtype)


def _flash_kernel(qt_ref, k_ref, vt_ref, o_ref, qpad_ref, m_ref, l_ref, acc_ref):
    kv = pl.program_id(2)
    tq = qt_ref.shape[2]
    hd = HEAD_DIM
    group = ATT_HEADS // ATT_KV_HEADS

    @pl.when(kv == 0)
    def _():
        zeros = jnp.zeros((hd, tq), qpad_ref.dtype)
        for h in range(ATT_HEADS):
            qh = qt_ref[0, h * hd:(h + 1) * hd, :]
            if (h // group) % 2 == 0:
                qpad_ref[h] = jnp.concatenate([qh, zeros], axis=0)
            else:
                qpad_ref[h] = jnp.concatenate([zeros, qh], axis=0)
        m_ref[...] = jnp.full_like(m_ref, -jnp.inf)
        l_ref[...] = jnp.zeros_like(l_ref)
        acc_ref[...] = jnp.zeros_like(acc_ref)

    def scores(h):
        a = h // (2 * group)
        return _dot(k_ref[0, :, a * LANES:(a + 1) * LANES], qpad_ref[h])

    st_next = scores(0)
    for h in range(ATT_HEADS):
        st = st_next
        if h + 1 < ATT_HEADS:
            st_next = scores(h + 1)
        g = h // group
        vt = vt_ref[0, g * hd:(g + 1) * hd, :]
        m_prev = m_ref[h]
        m_new = jnp.maximum(m_prev, jnp.max(st, axis=0, keepdims=True))
        alpha = jnp.exp2(m_prev - m_new)
        pt = jnp.exp2(st - m_new)
        l_ref[h] = alpha * l_ref[h] + jnp.sum(pt, axis=0, keepdims=True)
        m_ref[h] = m_new
        rows = slice(h * hd, (h + 1) * hd)
        acc_ref[rows] = acc_ref[rows] * alpha + _dot(vt, pt.astype(BF16))

    @pl.when(kv == pl.num_programs(2) - 1)
    def _():
        for h in range(ATT_HEADS):
            rows = slice(h * hd, (h + 1) * hd)
            acc_ref[rows] = acc_ref[rows] * (1.0 / l_ref[h])
        for p in range(ATT_QW // LANES):
            sl = slice(p * LANES, (p + 1) * LANES)
            o_ref[0, :, sl] = acc_ref[sl].T.astype(o_ref.dtype)


def _rope_tables(n):
    rows = n // GRID_W
    row = jnp.repeat(jnp.arange(rows, dtype=F32), GRID_W)
    colp = jnp.tile(jnp.arange(GRID_W, dtype=F32), rows)
    inv_freq = 1.0 / (ROPE_THETA ** (jnp.arange(0, ROPE_HALF, 2, dtype=F32) / ROPE_HALF))
    ang_r = row[:, None] * inv_freq
    ang_c = colp[:, None] * inv_freq
    cos64 = jnp.concatenate([jnp.cos(ang_r), jnp.cos(ang_r), jnp.cos(ang_c), jnp.cos(ang_c)], axis=-1)
    sin64 = jnp.concatenate([-jnp.sin(ang_r), jnp.sin(ang_r), -jnp.sin(ang_c), jnp.sin(ang_c)], axis=-1)
    return jnp.tile(cos64, (1, 2)), jnp.tile(sin64, (1, 2))


def _gqa_axial(z, g_q, g_k, cos_t, sin_t, tm_pref=512, tq_pref=1024, tk_pref=1024):
    b, n, _ = z.shape
    tm = _tile(n, tm_pref)
    seg = lax.broadcasted_iota(jnp.int32, (LANES, LANES), 0) // HEAD_DIM
    seg_t = lax.broadcasted_iota(jnp.int32, (LANES, LANES), 1) // HEAD_DIM
    avg = jnp.where(seg == seg_t, 1.0 / HEAD_DIM, 0.0).astype(BF16)
    gq2 = jnp.tile(g_q.reshape(1, HEAD_DIM).astype(F32), (1, 2))
    gk2 = jnp.tile(g_k.reshape(1, HEAD_DIM).astype(F32), (1, 2))
    vec = pl.BlockSpec((1, LANES), lambda bi, i: (0, 0))
    tab = pl.BlockSpec((tm, LANES), lambda bi, i: (i, 0))
    qt, kh, vt = pl.pallas_call(
        _attn_prep_kernel,
        grid=(b, n // tm),
        in_specs=[
            pl.BlockSpec((1, tm, ATT_QW), lambda bi, i: (bi, i, OFF_AQ // ATT_QW)),
            pl.BlockSpec((1, tm, ATT_KVW), lambda bi, i: (bi, i, OFF_AK // ATT_KVW)),
            pl.BlockSpec((1, tm, ATT_KVW), lambda bi, i: (bi, i, OFF_AV // ATT_KVW)),
            tab, tab, vec, vec,
            pl.BlockSpec((LANES, LANES), lambda bi, i: (0, 0)),
        ],
        out_specs=[
            pl.BlockSpec((1, ATT_QW, tm), lambda bi, i: (bi, 0, i)),
            pl.BlockSpec((1, tm, ATT_KVW), lambda bi, i: (bi, i, 0)),
            pl.BlockSpec((1, ATT_KVW, tm), lambda bi, i: (bi, 0, i)),
        ],
        out_shape=[
            jax.ShapeDtypeStruct((b, ATT_QW, n), BF16),
            jax.ShapeDtypeStruct((b, n, ATT_KVW), BF16),
            jax.ShapeDtypeStruct((b, ATT_KVW, n), BF16),
        ],
        compiler_params=_cparams("parallel", "parallel"),
        name="attn_prep",
    )(z, z, z, cos_t, sin_t, gq2, gk2, avg)

    tq = _tile(n, tq_pref)
    tk = _tile(n, tk_pref)
    return pl.pallas_call(
        _flash_kernel,
        grid=(b, n // tq, n // tk),
        in_specs=[
            pl.BlockSpec((1, ATT_QW, tq), lambda bi, i, j: (bi, 0, i)),
            pl.BlockSpec((1, tk, ATT_KVW), lambda bi, i, j: (bi, j, 0)),
            pl.BlockSpec((1, ATT_KVW, tk), lambda bi, i, j: (bi, 0, j)),
        ],
        out_specs=pl.BlockSpec((1, tq, ATT_QW), lambda bi, i, j: (bi, i, 0)),
        out_shape=jax.ShapeDtypeStruct((b, n, ATT_QW), BF16),
        scratch_shapes=[
            pltpu.VMEM((ATT_HEADS, LANES, tq), BF16),
            pltpu.VMEM((ATT_HEADS, 1, tq), F32),
            pltpu.VMEM((ATT_HEADS, 1, tq), F32),
            pltpu.VMEM((ATT_QW, tq), F32),
        ],
        compiler_params=_cparams("parallel", "parallel", "arbitrary"),
        name="flash_gqa",
    )(qt, kh, vt)


def _merge_kernel(x_ref, yp_ref, yh_ref, ya_ref, ga_ref, gb_ref, wp_ref, wh_ref, wa_ref, wo_ref, o_ref):
    ga = jax.nn.sigmoid(ga_ref[...].astype(F32))
    gb = jax.nn.sigmoid(gb_ref[...].astype(F32))
    d = D_MODEL
    g_pool = ga[:, :d]
    g_hg = jnp.concatenate([ga[:, d:], gb[:, :2 * d - GATE_BLK]], axis=1)
    g_att = gb[:, 2 * d - GATE_BLK:]
    merged = (g_pool * _dot(yp_ref[...], wp_ref[...])
              + g_hg * _dot(yh_ref[...], wh_ref[...])
              + g_att * _dot(ya_ref[...], wa_ref[...]))
    o_ref[...] = x_ref[...] + _dot(merged.astype(BF16), wo_ref[...])


def _merge(x, y_pool, y_hg, y_att, z2, w_p, w_h, w_a, w_o, tm_pref=512):
    t, d = x.shape
    tm = _tile(t, tm_pref)

    def rowblk(w):
        return pl.BlockSpec((tm, w), lambda i: (i, 0))

    def whole(w):
        return pl.BlockSpec(w.shape, lambda i: (0, 0))

    return pl.pallas_call(
        _merge_kernel,
        grid=(t // tm,),
        in_specs=[
            rowblk(d), rowblk(POOL_WIDTH), rowblk(HG_W), rowblk(ATT_QW),
            pl.BlockSpec((tm, GATE_BLK), lambda i: (i, OFF_GATE // GATE_BLK)),
            pl.BlockSpec((tm, GATE_BLK), lambda i: (i, OFF_GATE // GATE_BLK + 1)),
            whole(w_p), whole(w_h), whole(w_a), whole(w_o),
        ],
        out_specs=rowblk(d),
        out_shape=jax.ShapeDtypeStruct((t, d), F32),
        compiler_params=_cparams("parallel"),
        name="merge_out_proj",
    )(x, y_pool, y_hg, y_att, z2, z2, w_p, w_h, w_a, w_o)


def _gelu_tanh(a):
    half = 0.5 * a
    inner = a * (0.7978845608028654 + (0.7978845608028654 * 0.044715) * (a * a))
    return half + half * jnp.tanh(inner)


def _ffn_kernel(x_ref, prev_ref, next_ref, g_ref, wu_ref, cw_ref, cb_ref, wd_ref, o_ref, h_ref, *, n, tm, tf):
    i = pl.program_id(1)
    hl = FFN_HALO
    g = g_ref[...]
    h_ref[hl:hl + tm] = _rms(x_ref[0], g).astype(BF16)
    h_ref[0:hl] = jnp.where(i > 0, _rms(prev_ref[0], g), 0.0).astype(BF16)
    h_ref[hl + tm:] = jnp.where((i + 1) * tm < n, _rms(next_ref[0], g), 0.0).astype(BF16)
    h = h_ref[...]
    nf = D_FF // tf

    def up(j):
        return tuple(_dot(h, wu_ref[:, c0 + j * tf:c0 + (j + 1) * tf]) for c0 in (0, D_FF))

    rows = tm + 2 * hl

    def conv(u, c0):
        cw = cw_ref[:, c0:c0 + tf]
        prev = pltpu.roll(u, 1, 0)[hl:hl + tm]
        nxt_ = pltpu.roll(u, rows - 1, 0)[hl:hl + tm]
        return prev * cw[0:1] + u[hl:hl + tm] * cw[1:2] + nxt_ * cw[2:3] + cb_ref[:, c0:c0 + tf]

    acc = None
    acts = []
    nxt = up(0)
    for j in range(nf):
        ua, ug = nxt
        if j + 1 < nf:
            nxt = up(j + 1)
        acts.append((_gelu_tanh(conv(ua, j * tf)) * conv(ug, D_FF + j * tf)).astype(BF16))
        if len(acts) == FFN_DOWN_GROUP or j == nf - 1:
            k0 = (j + 1 - len(acts)) * tf
            part = _dot(jnp.concatenate(acts, axis=1), wd_ref[k0:(j + 1) * tf, :])
            acc = part if acc is None else acc + part
            acts = []
    o_ref[0] = x_ref[0] + acc


def _ffn(x, g, w_up, conv_w, conv_b, w_down, tm_pref=512, tf_pref=256):
    b, n, d = x.shape
    tm = _tile(n, tm_pref)
    tf = _tile(D_FF, tf_pref)
    hb = tm // FFN_HALO
    nhb = n // FFN_HALO

    def whole(a):
        return pl.BlockSpec(a.shape, lambda bi, i: (0,) * a.ndim)

    return pl.pallas_call(
        functools.partial(_ffn_kernel, n=n, tm=tm, tf=tf),
        grid=(b, n // tm),
        in_specs=[
            pl.BlockSpec((1, tm, d), lambda bi, i: (bi, i, 0)),
            pl.BlockSpec((1, FFN_HALO, d), lambda bi, i: (bi, jnp.maximum(i * hb - 1, 0), 0)),
            pl.BlockSpec((1, FFN_HALO, d), lambda bi, i: (bi, jnp.minimum((i + 1) * hb, nhb - 1), 0)),
            whole(g), whole(w_up), whole(conv_w), whole(conv_b), whole(w_down),
        ],
        out_specs=pl.BlockSpec((1, tm, d), lambda bi, i: (bi, i, 0)),
        out_shape=jax.ShapeDtypeStruct((b, n, d), F32),
        scratch_shapes=[pltpu.VMEM((tm + 2 * FFN_HALO, d), BF16)],
        compiler_params=_cparams("parallel", "parallel"),
        name="conv_glu_ffn",
    )(x, x, x, g, w_up, conv_w, conv_b, w_down)


def _ple_kernel(x_ref, p_ref, g_ref, wg_ref, wp_ref, gf_ref, o_ref, *, final):
    x = x_ref[...]
    gate = jax.nn.sigmoid(_dot(_rms(x, g_ref[...]).astype(BF16), wg_ref[...]))
    y = x + gate * _dot(p_ref[...].astype(BF16), wp_ref[...])
    if final:
        y = _rms(y, gf_ref[...])
    o_ref[...] = y


def _ple(x, p, g, w_gate, w_ple, g_final, final, tm_pref=512):
    t, d = x.shape
    tm = _tile(t, tm_pref)
    return pl.pallas_call(
        functools.partial(_ple_kernel, final=final),
        grid=(t // tm,),
        in_specs=[
            pl.BlockSpec((tm, d), lambda i: (i, 0)),
            pl.BlockSpec((tm, PLE_DIM), lambda i: (i, 0)),
            pl.BlockSpec((1, d), lambda i: (0, 0)),
            pl.BlockSpec((d, d), lambda i: (0, 0)),
            pl.BlockSpec((PLE_DIM, d), lambda i: (0, 0)),
            pl.BlockSpec((1, d), lambda i: (0, 0)),
        ],
        out_specs=pl.BlockSpec((tm, d), lambda i: (i, 0)),
        out_shape=jax.ShapeDtypeStruct((t, d), F32),
        compiler_params=_cparams("parallel"),
        name="ple_final_norm" if final else "ple",
    )(x, p, g, w_gate, w_ple, g_final)


def _lower_bound(lb_raw):
    s = jnp.cumsum(jax.nn.softmax(lb_raw.astype(F32), axis=0), axis=0)
    return s - s[0:1]


def _trunk(x, p, wts):
    b, n, d = x.shape
    t = b * n
    depth = wts["w_in"].shape[0]
    cos_t, sin_t = _rope_tables(n)
    x2 = x.reshape(t, d)
    for i in range(depth):
        z2 = _norm_matmul(x2, wts["g_mix"][i], wts["w_in"][i])
        z = z2.reshape(b, n, IN_WIDTH)
        y_pool = _pool_mixer(z, wts["pool_w"][i], wts["pool_scale"][i])
        y_hg = _hgrn2_mixer(z, wts["lb_f"][i], wts["lb_b"][i], wts["hg_onorm"][i])
        y_att = _gqa_axial(z, wts["g_q"][i], wts["g_k"][i], cos_t, sin_t)
        x2 = _merge(x2, y_pool.reshape(t, POOL_WIDTH), y_hg.reshape(t, HG_W), y_att.reshape(t, ATT_QW), z2,
                    wts["w_br_pool"][i], wts["w_br_hg"][i], wts["w_br_att"][i], wts["w_out"][i])
        x2 = _ffn(x2.reshape(b, n, d), wts["g_ffn"][i], wts["w_up"][i], wts["conv_w"][i], wts["conv_b"][i],
                  wts["w_down"][i]).reshape(t, d)
        x2 = _ple(x2, p[i].reshape(t, PLE_DIM), wts["g_ple"][i], wts["w_ple_gate"][i], wts["w_ple"][i],
                  wts["g_final"], final=(i == depth - 1))
    return x2.reshape(b, n, d)


def kernel(x_prompt, x_sample, p_prompt, p_sample, g_mix, w_in, pool_w, pool_scale, lb_raw_f, lb_raw_b, hg_onorm,
           g_q, g_k, w_br_pool, w_br_hg, w_br_att, w_out, g_ffn, w_up, conv_w, conv_b, w_down, g_ple, w_ple_gate,
           w_ple, g_final):
    depth = w_in.shape[0]

    def vec(a):
        return a.astype(F32).reshape(depth, 1, a.shape[-1])

    wts = dict(
        g_mix=vec(g_mix), w_in=w_in.astype(BF16), pool_w=pool_w.astype(BF16), pool_scale=vec(pool_scale),
        lb_f=_lower_bound(lb_raw_f), lb_b=_lower_bound(lb_raw_b), hg_onorm=hg_onorm, g_q=g_q, g_k=g_k,
        w_br_pool=w_br_pool.astype(BF16), w_br_hg=w_br_hg.astype(BF16), w_br_att=w_br_att.astype(BF16),
        w_out=w_out.astype(BF16), g_ffn=vec(g_ffn), w_up=w_up.astype(BF16), conv_w=conv_w.astype(F32),
        conv_b=vec(conv_b), w_down=w_down.astype(BF16), g_ple=vec(g_ple), w_ple_gate=w_ple_gate.astype(BF16),
        w_ple=w_ple.astype(BF16), g_final=g_final.astype(F32).reshape(1, -1),
    )
    return _trunk(x_prompt, p_prompt, wts), _trunk(x_sample, p_sample, wts)
```

```python
import functools

import jax
import jax.numpy as jnp
from jax import lax
from jax.experimental import pallas as pl
from jax.experimental.pallas import tpu as pltpu

F32 = jnp.float32
BF16 = jnp.bfloat16

EPS = 1e-6
D_MODEL = 1024
GRID_W = 64
PLE_DIM = 256
POOL_WIDTH = 512
POOL_GROUPS = 4
POOL_GDIM = POOL_WIDTH // POOL_GROUPS
POOL_HALF = (1, 2, 4, 8)
POOL_HALO = 8
HALO_BLK = 16
HG_HEADS = 4
HG_DK = 128
HG_DV = 128
HG_W = HG_HEADS * HG_DK
HG_CHUNK = 64
HG_SUB = 16
ATT_HEADS = 16
ATT_KV_HEADS = 4
HEAD_DIM = 64
ATT_QW = ATT_HEADS * HEAD_DIM
ATT_KVW = ATT_KV_HEADS * HEAD_DIM
ROPE_THETA = 10000.0
ROPE_HALF = HEAD_DIM // 2
D_FF = 2816
FFN_HALO = 8
FFN_DOWN_GROUP = 4

OFF_POOL = 0
OFF_HQ = 512
OFF_HFF = 1024
OFF_HFB = 1536
OFF_HI = 2048
OFF_HG = 2560
OFF_AQ = 3072
OFF_AK = 4096
OFF_AV = 4352
OFF_GATE = 4608
IN_WIDTH = 7680
GATE_BLK = 1536

LANES = 128
LOG2E = 1.4426950408889634
SCORE_BOUND_SLACK = 1.02
MAX_BOUNDED_SCORE = 30.0
VMEM_LIMIT = 56 * 1024 * 1024


def _cparams(*sem):
    return pltpu.CompilerParams(dimension_semantics=sem, vmem_limit_bytes=VMEM_LIMIT)


def _tile(n, pref):
    t = min(n, pref)
    while n % t:
        t //= 2
    return t


def _dot(a, b):
    return jnp.dot(a, b, preferred_element_type=F32)


def _dot_nt(a, b):
    return lax.dot_general(a, b, (((1,), (1,)), ((), ())), preferred_element_type=F32)


def _dot_tn(a, b):
    return lax.dot_general(a, b, (((0,), (0,)), ((), ())), preferred_element_type=F32)


def _rms(x, g):
    return x * lax.rsqrt(jnp.mean(x * x, axis=-1, keepdims=True) + EPS) * g


def _norm_matmul_kernel(x_ref, g_ref, w_ref, o_ref, *, tn):
    h = _rms(x_ref[...], g_ref[...]).astype(BF16)
    for j in range(w_ref.shape[1] // tn):
        sl = slice(j * tn, (j + 1) * tn)
        o_ref[:, sl] = _dot(h, w_ref[:, sl]).astype(o_ref.dtype)


def _norm_matmul(x, g, w, out_dtype=BF16, tm_pref=512, tn_pref=512):
    t, k = x.shape
    nout = w.shape[1]
    tm = _tile(t, tm_pref)
    tn = _tile(nout, tn_pref)
    return pl.pallas_call(
        functools.partial(_norm_matmul_kernel, tn=tn),
        grid=(t // tm,),
        in_specs=[
            pl.BlockSpec((tm, k), lambda i: (i, 0)),
            pl.BlockSpec((1, k), lambda i: (0, 0)),
            pl.BlockSpec((k, nout), lambda i: (0, 0)),
        ],
        out_specs=pl.BlockSpec((tm, nout), lambda i: (i, 0)),
        out_shape=jax.ShapeDtypeStruct((t, nout), out_dtype),
        compiler_params=_cparams("parallel"),
        name="norm_in_proj",
    )(x, g, w)


def _pool_kernel(cur_ref, prev_ref, next_ref, w_ref, scale_ref, o_ref, *, n, tn):
    i = pl.program_id(1)
    t0 = i * tn
    u = cur_ref[0].astype(F32)
    hl = POOL_HALO
    prev = jnp.where(i > 0, prev_ref[0, HALO_BLK - hl:].astype(F32), 0.0)
    nxt = jnp.where(t0 + tn < n, next_ref[0, :hl].astype(F32), 0.0)
    e = jnp.concatenate([prev, u, nxt], axis=0)
    s2 = e[0:-1] + e[1:]
    c1 = POOL_GDIM
    s4 = s2[0:-2, c1:] + s2[2:, c1:]
    s8 = s4[0:-4, c1:] + s4[4:, c1:]
    s16 = s8[0:-8, c1:] + s8[8:, c1:]
    sums = (s2[7:7 + tn, :c1], s4[6:6 + tn, :c1], s8[4:4 + tn, :c1], s16[0:tn, :])
    t = t0 + lax.broadcasted_iota(jnp.int32, (tn, 1), 0)
    outs = []
    for g in range(POOL_GROUPS):
        h = POOL_HALF[g]
        cnt = (jnp.minimum(t + h, n) - jnp.maximum(t - h, 0)).astype(F32)
        ug = u[:, g * c1:(g + 1) * c1]
        mixed = (sums[g] / cnt - ug).astype(BF16)
        outs.append(_dot(mixed, w_ref[g]))
    y = jnp.concatenate(outs, axis=1) * scale_ref[...]
    o_ref[0] = y.astype(o_ref.dtype)


def _pool_mixer(z, pool_w, pool_scale, tn_pref=512):
    b, n, _ = z.shape
    tn = _tile(n, tn_pref)
    hb = tn // HALO_BLK
    nhb = n // HALO_BLK
    return pl.pallas_call(
        functools.partial(_pool_kernel, n=n, tn=tn),
        grid=(b, n // tn),
        in_specs=[
            pl.BlockSpec((1, tn, POOL_WIDTH), lambda bi, i: (bi, i, OFF_POOL // POOL_WIDTH)),
            pl.BlockSpec((1, HALO_BLK, POOL_WIDTH),
                         lambda bi, i: (bi, jnp.maximum(i * hb - 1, 0), OFF_POOL // POOL_WIDTH)),
            pl.BlockSpec((1, HALO_BLK, POOL_WIDTH),
                         lambda bi, i: (bi, jnp.minimum((i + 1) * hb, nhb - 1), OFF_POOL // POOL_WIDTH)),
            pl.BlockSpec((POOL_GROUPS, POOL_GDIM, POOL_GDIM), lambda bi, i: (0, 0, 0)),
            pl.BlockSpec((1, POOL_WIDTH), lambda bi, i: (0, 0)),
        ],
        out_specs=pl.BlockSpec((1, tn, POOL_WIDTH), lambda bi, i: (bi, i, 0)),
        out_shape=jax.ShapeDtypeStruct((b, n, POOL_WIDTH), BF16),
        compiler_params=_cparams("parallel", "parallel"),
        name="pool_mixer",
    )(z, z, z, pool_w, pool_scale)


def _split3(x):
    hi = x.astype(BF16)
    r1 = x - hi.astype(F32)
    mid = r1.astype(BF16)
    lo = (r1 - mid.astype(F32)).astype(BF16)
    return hi, mid, lo


def _hgrn_chunk(zq, zf, v, log_lb, log1m_lb, one_m_lb, st_ref, rev):
    c = HG_CHUNK
    zq, zf, v = zq.astype(F32), zf.astype(F32), v.astype(F32)
    q = zq * (HG_DK ** -0.5)
    ez = jnp.exp(-jnp.abs(zf))
    log_sig = jnp.minimum(zf, 0.0) - jnp.log1p(ez)
    k = one_m_lb * (jnp.where(zf >= 0.0, ez, 1.0) / (1.0 + ez))
    ga = log_lb
    gb = log1m_lb + log_sig
    logf = jnp.maximum(ga, gb) + jnp.log1p(jnp.exp(-jnp.abs(ga - gb)))

    row = lax.broadcasted_iota(jnp.int32, (c, c), 0)
    col = lax.broadcasted_iota(jnp.int32, (c, c), 1)
    keep = (col >= row) if rev else (col <= row)
    tri = jnp.where(keep, 1.0, 0.0).astype(BF16)
    hi, mid, lo = _split3(logf)
    cum = (_dot(tri, hi) + _dot(tri, mid) + _dot(tri, lo)) * LOG2E
    last = cum[0:1] if rev else cum[c - 1:c]
    qhat = (q * jnp.exp2(cum)).astype(BF16)
    khat = (k * jnp.exp2(last - cum)).astype(BF16)
    dec = jnp.exp2(last)

    nsub = c // HG_SUB
    half = HG_SUB // 2
    heads = [slice(h * HG_DK, (h + 1) * HG_DK) for h in range(HG_HEADS)]
    a_rows = [[] for _ in range(HG_HEADS)]
    col_h = lax.broadcasted_iota(jnp.int32, (half, c), 1)
    for jb in range(nsub):
        r0 = jb * HG_SUB
        lo_r, hi_r = (r0 + HG_SUB, c) if rev else (0, r0)
        if hi_r > lo_r:
            ridx = lo_r if rev else hi_r - 1
            ref_row = cum[ridx:ridx + 1]
            qt = (q[r0:r0 + HG_SUB] * jnp.exp2(cum[r0:r0 + HG_SUB] - ref_row)).astype(BF16)
            pieces = [jnp.zeros((lo_r, HG_W), BF16),
                      (k[lo_r:hi_r] * jnp.exp2(ref_row - cum[lo_r:hi_r])).astype(BF16),
                      jnp.zeros((c - hi_r, HG_W), BF16)]
            kt = jnp.concatenate([p for p in pieces if p.shape[0]], axis=0)
            offs = [_dot_nt(qt[:, sl], kt[:, sl]) for sl in heads]
            halves = [[o[:half] for o in offs], [o[half:] for o in offs]]
        else:
            halves = [[jnp.zeros((half, c), F32) for _ in heads] for _ in range(2)]
        for l in range(HG_SUB):
            lr = r0 + l
            for which in range(2):
                if (which == 0 and not rev and l >= half) or (which == 1 and rev and l < half):
                    continue
                rr = r0 + which * half
                x = q[rr:rr + half] * jnp.exp2(jnp.minimum(cum[rr:rr + half] - cum[lr:lr + 1], 0.0)) * k[lr:lr + 1]
                for h, sl in enumerate(heads):
                    s = jnp.sum(x[:, sl], axis=-1, keepdims=True)
                    halves[which][h] = jnp.where(col_h == lr, s, halves[which][h])
        for h in range(HG_HEADS):
            a_rows[h] += [halves[0][h], halves[1][h]]

    outs = []
    for h, sl in enumerate(heads):
        a = jnp.where(keep, jnp.concatenate(a_rows[h], axis=0), 0.0).astype(BF16)
        vh = v[:, sl].astype(BF16)
        st = st_ref[h]
        outs.append(_dot(a, vh) + _dot_nt(qhat[:, sl], st.astype(BF16)))
        st_ref[h] = st * dec[:, sl] + _dot_tn(vh, khat[:, sl])
    return jnp.concatenate(outs, axis=1)


def _hgrn_kernel(*refs, rev, combine, chunks):
    if combine:
        zq_ref, zf_ref, zi_ref, zg_ref, ofw_ref, llb_ref, l1m_ref, oml_ref, gon_ref, o_ref, st_ref = refs
    else:
        zq_ref, zf_ref, zi_ref, llb_ref, l1m_ref, oml_ref, o_ref, st_ref = refs

    @pl.when(pl.program_id(1) == 0)
    def _():
        st_ref[...] = jnp.zeros_like(st_ref)

    order = range(chunks - 1, -1, -1) if rev else range(chunks)
    for ci in order:
        rows = slice(ci * HG_CHUNK, (ci + 1) * HG_CHUNK)
        o = _hgrn_chunk(zq_ref[0, rows], zf_ref[0, rows], zi_ref[0, rows], llb_ref[...], l1m_ref[...],
                        oml_ref[...], st_ref, rev)
        if combine:
            o = o + ofw_ref[0, rows]
            zg = zg_ref[0, rows].astype(F32)
            y = jnp.concatenate([_rms(o[:, h * HG_DV:(h + 1) * HG_DV], gon_ref[...]) for h in range(HG_HEADS)],
                                axis=1)
            o = y * (zg * jax.nn.sigmoid(zg))
        o_ref[0, rows] = o.astype(o_ref.dtype)


def _hgrn2_mixer(z, lb_f, lb_b, g_onorm, chunks_pref=4):
    b, n, _ = z.shape
    chunks = chunks_pref if (n // HG_CHUNK) % chunks_pref == 0 else 1
    rows_blk = HG_CHUNK * chunks
    nblk = n // rows_blk

    def lb_rows(lbv):
        lbv = lbv.reshape(1, HG_W).astype(F32)
        return jnp.log(lbv), jnp.log1p(-lbv), 1.0 - lbv

    def spec(col, rev):
        if rev:
            return pl.BlockSpec((1, rows_blk, HG_W), lambda bi, i: (bi, nblk - 1 - i, col))
        return pl.BlockSpec((1, rows_blk, HG_W), lambda bi, i: (bi, i, col))

    vec = pl.BlockSpec((1, HG_W), lambda bi, i: (0, 0))
    scratch = [pltpu.VMEM((HG_HEADS, HG_DV, HG_DK), F32)]
    o_fw = pl.pallas_call(
        functools.partial(_hgrn_kernel, rev=False, combine=False, chunks=chunks),
        grid=(b, nblk),
        in_specs=[spec(OFF_HQ // HG_W, False), spec(OFF_HFF // HG_W, False), spec(OFF_HI // HG_W, False),
                  vec, vec, vec],
        out_specs=spec(0, False),
        out_shape=jax.ShapeDtypeStruct((b, n, HG_W), F32),
        scratch_shapes=scratch,
        compiler_params=_cparams("parallel", "arbitrary"),
        name="hgrn2_forward",
    )(z, z, z, *lb_rows(lb_f))
    return pl.pallas_call(
        functools.partial(_hgrn_kernel, rev=True, combine=True, chunks=chunks),
        grid=(b, nblk),
        in_specs=[spec(OFF_HQ // HG_W, True), spec(OFF_HFB // HG_W, True), spec(OFF_HI // HG_W, True),
                  spec(OFF_HG // HG_W, True), spec(0, True), vec, vec, vec,
                  pl.BlockSpec((1, HG_DV), lambda bi, i: (0, 0))],
        out_specs=spec(0, True),
        out_shape=jax.ShapeDtypeStruct((b, n, HG_W), BF16),
        scratch_shapes=scratch,
        compiler_params=_cparams("parallel", "arbitrary"),
        name="hgrn2_backward_combine",
    )(z, z, z, z, o_fw, *lb_rows(lb_b), g_onorm.reshape(1, HG_DV).astype(F32))


def _headnorm_rope(x, avg, g, cosv, sinv):
    hi = (x * x).astype(BF16)
    lo = (x * x - hi.astype(F32)).astype(BF16)
    ms = _dot(hi, avg) + _dot(lo, avg)
    y = x * lax.rsqrt(ms + EPS) * g
    lane = lax.broadcasted_iota(jnp.int32, y.shape, 1)
    first = (lane % ROPE_HALF) < (ROPE_HALF // 2)
    partner = jnp.where(first, pltpu.roll(y, LANES - ROPE_HALF // 2, 1), pltpu.roll(y, ROPE_HALF // 2, 1))
    return y * cosv + partner * sinv


def _attn_prep_kernel(zq_ref, zk_ref, zv_ref, cos_ref, sin_ref, gq_ref, gk_ref, avg_ref,
                      qt_ref, k_ref, vt_ref, qn_ref, kn_ref):
    cosv = cos_ref[...]
    sinv = sin_ref[...]
    avg = avg_ref[...]

    def max_mean_square(yb):
        y32 = yb.astype(F32)
        ms = _dot((y32 * y32).astype(BF16), avg)
        return jnp.broadcast_to(jnp.max(ms, axis=0, keepdims=True), (8, LANES))

    for p in range(ATT_QW // LANES):
        sl = slice(p * LANES, (p + 1) * LANES)
        y = _headnorm_rope(zq_ref[0, :, sl].astype(F32), avg, gq_ref[...], cosv, sinv) * (HEAD_DIM ** -0.5 * LOG2E)
        yb = y.astype(qt_ref.dtype)
        qn_ref[0, 0, :, sl] = max_mean_square(yb)
        qt_ref[0, sl, :] = y.T.astype(qt_ref.dtype)
    for p in range(ATT_KVW // LANES):
        sl = slice(p * LANES, (p + 1) * LANES)
        kb = _headnorm_rope(zk_ref[0, :, sl].astype(F32), avg, gk_ref[...], cosv, sinv).astype(k_ref.dtype)
        kn_ref[0, 0, :, sl] = max_mean_square(kb)
        k_ref[0, :, sl] = kb
        vt_ref[0, sl, :] = zv_ref[0, :, sl].astype(F32).T.astype(vt_ref.dtype)


def _flash_kernel(bound_ref, fast_ref, qt_ref, k_ref, vt_ref, o_ref, qpad_ref, m_ref, l_ref, acc_ref):
    bi = pl.program_id(0)
    kv = pl.program_id(2)
    tq = qt_ref.shape[2]
    hd = HEAD_DIM
    group = ATT_HEADS // ATT_KV_HEADS

    @pl.when(kv == 0)
    def _():
        zeros = jnp.zeros((hd, tq), qpad_ref.dtype)
        for h in range(ATT_HEADS):
            qh = qt_ref[0, h * hd:(h + 1) * hd, :]
            if (h // group) % 2 == 0:
                qpad_ref[h] = jnp.concatenate([qh, zeros], axis=0)
            else:
                qpad_ref[h] = jnp.concatenate([zeros, qh], axis=0)
        m_ref[...] = jnp.full_like(m_ref, -jnp.inf)
        l_ref[...] = jnp.zeros_like(l_ref)
        acc_ref[...] = jnp.zeros_like(acc_ref)

    def scores(h):
        a = h // (2 * group)
        return _dot(k_ref[0, :, a * LANES:(a + 1) * LANES], qpad_ref[h])

    def all_heads(update):
        st_next = scores(0)
        for h in range(ATT_HEADS):
            st = st_next
            if h + 1 < ATT_HEADS:
                st_next = scores(h + 1)
            g = h // group
            update(h, st, vt_ref[0, g * hd:(g + 1) * hd, :], slice(h * hd, (h + 1) * hd))

    def update_bounded(h, st, vt, rows):
        pt = jnp.exp2(st - bound_ref[bi, h])
        l_ref[h] = l_ref[h] + jnp.sum(pt, axis=0, keepdims=True)
        acc_ref[rows] = acc_ref[rows] + _dot(vt, pt.astype(BF16))

    def update_running_max(h, st, vt, rows):
        m_prev = m_ref[h]
        m_new = jnp.maximum(m_prev, jnp.max(st, axis=0, keepdims=True))
        alpha = jnp.exp2(m_prev - m_new)
        pt = jnp.exp2(st - m_new)
        l_ref[h] = alpha * l_ref[h] + jnp.sum(pt, axis=0, keepdims=True)
        m_ref[h] = m_new
        acc_ref[rows] = acc_ref[rows] * alpha + _dot(vt, pt.astype(BF16))

    @pl.when(fast_ref[0] == 1)
    def _():
        all_heads(update_bounded)

    @pl.when(fast_ref[0] != 1)
    def _():
        all_heads(update_running_max)

    @pl.when(kv == pl.num_programs(2) - 1)
    def _():
        for h in range(ATT_HEADS):
            rows = slice(h * hd, (h + 1) * hd)
            acc_ref[rows] = acc_ref[rows] * (1.0 / l_ref[h])
        for p in range(ATT_QW // LANES):
            sl = slice(p * LANES, (p + 1) * LANES)
            o_ref[0, :, sl] = acc_ref[sl].T.astype(o_ref.dtype)


def _rope_tables(n):
    rows = n // GRID_W
    row = jnp.repeat(jnp.arange(rows, dtype=F32), GRID_W)
    colp = jnp.tile(jnp.arange(GRID_W, dtype=F32), rows)
    inv_freq = 1.0 / (ROPE_THETA ** (jnp.arange(0, ROPE_HALF, 2, dtype=F32) / ROPE_HALF))
    ang_r = row[:, None] * inv_freq
    ang_c = colp[:, None] * inv_freq
    cos64 = jnp.concatenate([jnp.cos(ang_r), jnp.cos(ang_r), jnp.cos(ang_c), jnp.cos(ang_c)], axis=-1)
    sin64 = jnp.concatenate([-jnp.sin(ang_r), jnp.sin(ang_r), -jnp.sin(ang_c), jnp.sin(ang_c)], axis=-1)
    return jnp.tile(cos64, (1, 2)), jnp.tile(sin64, (1, 2))


def _gqa_axial(z, g_q, g_k, cos_t, sin_t, tm_pref=512, tq_pref=1024, tk_pref=1024):
    b, n, _ = z.shape
    tm = _tile(n, tm_pref)
    seg = lax.broadcasted_iota(jnp.int32, (LANES, LANES), 0) // HEAD_DIM
    seg_t = lax.broadcasted_iota(jnp.int32, (LANES, LANES), 1) // HEAD_DIM
    avg = jnp.where(seg == seg_t, 1.0 / HEAD_DIM, 0.0).astype(BF16)
    gq2 = jnp.tile(g_q.reshape(1, HEAD_DIM).astype(F32), (1, 2))
    gk2 = jnp.tile(g_k.reshape(1, HEAD_DIM).astype(F32), (1, 2))
    vec = pl.BlockSpec((1, LANES), lambda bi, i: (0, 0))
    tab = pl.BlockSpec((tm, LANES), lambda bi, i: (i, 0))
    qt, kh, vt, qn, kn = pl.pallas_call(
        _attn_prep_kernel,
        grid=(b, n // tm),
        in_specs=[
            pl.BlockSpec((1, tm, ATT_QW), lambda bi, i: (bi, i, OFF_AQ // ATT_QW)),
            pl.BlockSpec((1, tm, ATT_KVW), lambda bi, i: (bi, i, OFF_AK // ATT_KVW)),
            pl.BlockSpec((1, tm, ATT_KVW), lambda bi, i: (bi, i, OFF_AV // ATT_KVW)),
            tab, tab, vec, vec,
            pl.BlockSpec((LANES, LANES), lambda bi, i: (0, 0)),
        ],
        out_specs=[
            pl.BlockSpec((1, ATT_QW, tm), lambda bi, i: (bi, 0, i)),
            pl.BlockSpec((1, tm, ATT_KVW), lambda bi, i: (bi, i, 0)),
            pl.BlockSpec((1, ATT_KVW, tm), lambda bi, i: (bi, 0, i)),
            pl.BlockSpec((1, 1, 8, ATT_QW), lambda bi, i: (bi, i, 0, 0)),
            pl.BlockSpec((1, 1, 8, ATT_KVW), lambda bi, i: (bi, i, 0, 0)),
        ],
        out_shape=[
            jax.ShapeDtypeStruct((b, ATT_QW, n), BF16),
            jax.ShapeDtypeStruct((b, n, ATT_KVW), BF16),
            jax.ShapeDtypeStruct((b, ATT_KVW, n), BF16),
            jax.ShapeDtypeStruct((b, n // tm, 8, ATT_QW), F32),
            jax.ShapeDtypeStruct((b, n // tm, 8, ATT_KVW), F32),
        ],
        compiler_params=_cparams("parallel", "parallel"),
        name="attn_prep",
    )(z, z, z, cos_t, sin_t, gq2, gk2, avg)

    def max_norm(ms, heads):
        ms = jnp.max(ms[:, :, 0, :], axis=1).reshape(b, heads, HEAD_DIM)[:, :, 0]
        return jnp.sqrt(ms * HEAD_DIM)

    group = ATT_HEADS // ATT_KV_HEADS
    bound = SCORE_BOUND_SLACK * max_norm(qn, ATT_HEADS) * jnp.repeat(max_norm(kn, ATT_KV_HEADS), group, axis=1)
    fast = jnp.all(bound <= MAX_BOUNDED_SCORE).astype(jnp.int32).reshape(1)

    tq = _tile(n, tq_pref)
    tk = _tile(n, tk_pref)
    return pl.pallas_call(
        _flash_kernel,
        grid_spec=pltpu.PrefetchScalarGridSpec(
            num_scalar_prefetch=2,
            grid=(b, n // tq, n // tk),
            in_specs=[
                pl.BlockSpec((1, ATT_QW, tq), lambda bi, i, j, *_: (bi, 0, i)),
                pl.BlockSpec((1, tk, ATT_KVW), lambda bi, i, j, *_: (bi, j, 0)),
                pl.BlockSpec((1, ATT_KVW, tk), lambda bi, i, j, *_: (bi, 0, j)),
            ],
            out_specs=pl.BlockSpec((1, tq, ATT_QW), lambda bi, i, j, *_: (bi, i, 0)),
            scratch_shapes=[
                pltpu.VMEM((ATT_HEADS, LANES, tq), BF16),
                pltpu.VMEM((ATT_HEADS, 1, tq), F32),
                pltpu.VMEM((ATT_HEADS, 1, tq), F32),
                pltpu.VMEM((ATT_QW, tq), F32),
            ],
        ),
        out_shape=jax.ShapeDtypeStruct((b, n, ATT_QW), BF16),
        compiler_params=_cparams("parallel", "parallel", "arbitrary"),
        name="flash_gqa",
    )(bound.astype(F32), fast, qt, kh, vt)


def _merge_kernel(x_ref, yp_ref, yh_ref, ya_ref, ga_ref, gb_ref, wp_ref, wh_ref, wa_ref, wo_ref, o_ref):
    ga = jax.nn.sigmoid(ga_ref[...].astype(F32))
    gb = jax.nn.sigmoid(gb_ref[...].astype(F32))
    d = D_MODEL
    g_pool = ga[:, :d]
    g_hg = jnp.concatenate([ga[:, d:], gb[:, :2 * d - GATE_BLK]], axis=1)
    g_att = gb[:, 2 * d - GATE_BLK:]
    merged = (g_pool * _dot(yp_ref[...], wp_ref[...])
              + g_hg * _dot(yh_ref[...], wh_ref[...])
              + g_att * _dot(ya_ref[...], wa_ref[...]))
    o_ref[...] = x_ref[...] + _dot(merged.astype(BF16), wo_ref[...])


def _merge(x, y_pool, y_hg, y_att, z2, w_p, w_h, w_a, w_o, tm_pref=512):
    t, d = x.shape
    tm = _tile(t, tm_pref)

    def rowblk(w):
        return pl.BlockSpec((tm, w), lambda i: (i, 0))

    def whole(w):
        return pl.BlockSpec(w.shape, lambda i: (0, 0))

    return pl.pallas_call(
        _merge_kernel,
        grid=(t // tm,),
        in_specs=[
            rowblk(d), rowblk(POOL_WIDTH), rowblk(HG_W), rowblk(ATT_QW),
            pl.BlockSpec((tm, GATE_BLK), lambda i: (i, OFF_GATE // GATE_BLK)),
            pl.BlockSpec((tm, GATE_BLK), lambda i: (i, OFF_GATE // GATE_BLK + 1)),
            whole(w_p), whole(w_h), whole(w_a), whole(w_o),
        ],
        out_specs=rowblk(d),
        out_shape=jax.ShapeDtypeStruct((t, d), F32),
        compiler_params=_cparams("parallel"),
        name="merge_out_proj",
    )(x, y_pool, y_hg, y_att, z2, z2, w_p, w_h, w_a, w_o)


def _gelu_tanh(a):
    half = 0.5 * a
    inner = a * (0.7978845608028654 + (0.7978845608028654 * 0.044715) * (a * a))
    return half + half * jnp.tanh(inner)


def _ffn_kernel(x_ref, prev_ref, next_ref, g_ref, wu_ref, cw_ref, cb_ref, wd_ref, o_ref, h_ref, *, n, tm, tf):
    i = pl.program_id(1)
    hl = FFN_HALO
    g = g_ref[...]
    h_ref[hl:hl + tm] = _rms(x_ref[0], g).astype(BF16)
    h_ref[0:hl] = jnp.where(i > 0, _rms(prev_ref[0], g), 0.0).astype(BF16)
    h_ref[hl + tm:] = jnp.where((i + 1) * tm < n, _rms(next_ref[0], g), 0.0).astype(BF16)
    h = h_ref[...]
    nf = D_FF // tf

    def up(j):
        return tuple(_dot(h, wu_ref[:, c0 + j * tf:c0 + (j + 1) * tf]) for c0 in (0, D_FF))

    rows = tm + 2 * hl

    def conv(u, c0):
        cw = cw_ref[:, c0:c0 + tf]
        prev = pltpu.roll(u, 1, 0)[hl:hl + tm]
        nxt_ = pltpu.roll(u, rows - 1, 0)[hl:hl + tm]
        return prev * cw[0:1] + u[hl:hl + tm] * cw[1:2] + nxt_ * cw[2:3] + cb_ref[:, c0:c0 + tf]

    acc = None
    acts = []
    nxt = up(0)
    for j in range(nf):
        ua, ug = nxt
        if j + 1 < nf:
            nxt = up(j + 1)
        acts.append((_gelu_tanh(conv(ua, j * tf)) * conv(ug, D_FF + j * tf)).astype(BF16))
        if len(acts) == FFN_DOWN_GROUP or j == nf - 1:
            k0 = (j + 1 - len(acts)) * tf
            part = _dot(jnp.concatenate(acts, axis=1), wd_ref[k0:(j + 1) * tf, :])
            acc = part if acc is None else acc + part
            acts = []
    o_ref[0] = x_ref[0] + acc


def _ffn(x, g, w_up, conv_w, conv_b, w_down, tm_pref=512, tf_pref=256):
    b, n, d = x.shape
    tm = _tile(n, tm_pref)
    tf = _tile(D_FF, tf_pref)
    hb = tm // FFN_HALO
    nhb = n // FFN_HALO

    def whole(a):
        return pl.BlockSpec(a.shape, lambda bi, i: (0,) * a.ndim)

    return pl.pallas_call(
        functools.partial(_ffn_kernel, n=n, tm=tm, tf=tf),
        grid=(b, n // tm),
        in_specs=[
            pl.BlockSpec((1, tm, d), lambda bi, i: (bi, i, 0)),
            pl.BlockSpec((1, FFN_HALO, d), lambda bi, i: (bi, jnp.maximum(i * hb - 1, 0), 0)),
            pl.BlockSpec((1, FFN_HALO, d), lambda bi, i: (bi, jnp.minimum((i + 1) * hb, nhb - 1), 0)),
            whole(g), whole(w_up), whole(conv_w), whole(conv_b), whole(w_down),
        ],
        out_specs=pl.BlockSpec((1, tm, d), lambda bi, i: (bi, i, 0)),
        out_shape=jax.ShapeDtypeStruct((b, n, d), F32),
        scratch_shapes=[pltpu.VMEM((tm + 2 * FFN_HALO, d), BF16)],
        compiler_params=_cparams("parallel", "parallel"),
        name="conv_glu_ffn",
    )(x, x, x, g, w_up, conv_w, conv_b, w_down)


def _ple_kernel(x_ref, p_ref, g_ref, wg_ref, wp_ref, gf_ref, o_ref, *, final):
    x = x_ref[...]
    gate = jax.nn.sigmoid(_dot(_rms(x, g_ref[...]).astype(BF16), wg_ref[...]))
    y = x + gate * _dot(p_ref[...].astype(BF16), wp_ref[...])
    if final:
        y = _rms(y, gf_ref[...])
    o_ref[...] = y


def _ple(x, p, g, w_gate, w_ple, g_final, final, tm_pref=512):
    t, d = x.shape
    tm = _tile(t, tm_pref)
    return pl.pallas_call(
        functools.partial(_ple_kernel, final=final),
        grid=(t // tm,),
        in_specs=[
            pl.BlockSpec((tm, d), lambda i: (i, 0)),
            pl.BlockSpec((tm, PLE_DIM), lambda i: (i, 0)),
            pl.BlockSpec((1, d), lambda i: (0, 0)),
            pl.BlockSpec((d, d), lambda i: (0, 0)),
            pl.BlockSpec((PLE_DIM, d), lambda i: (0, 0)),
            pl.BlockSpec((1, d), lambda i: (0, 0)),
        ],
        out_specs=pl.BlockSpec((tm, d), lambda i: (i, 0)),
        out_shape=jax.ShapeDtypeStruct((t, d), F32),
        compiler_params=_cparams("parallel"),
        name="ple_final_norm" if final else "ple",
    )(x, p, g, w_gate, w_ple, g_final)


def _lower_bound(lb_raw):
    s = jnp.cumsum(jax.nn.softmax(lb_raw.astype(F32), axis=0), axis=0)
    return s - s[0:1]


def _trunk(x, p, wts):
    b, n, d = x.shape
    t = b * n
    depth = wts["w_in"].shape[0]
    cos_t, sin_t = _rope_tables(n)
    x2 = x.reshape(t, d)
    for i in range(depth):
        z2 = _norm_matmul(x2, wts["g_mix"][i], wts["w_in"][i])
        z = z2.reshape(b, n, IN_WIDTH)
        y_pool = _pool_mixer(z, wts["pool_w"][i], wts["pool_scale"][i])
        y_hg = _hgrn2_mixer(z, wts["lb_f"][i], wts["lb_b"][i], wts["hg_onorm"][i])
        y_att = _gqa_axial(z, wts["g_q"][i], wts["g_k"][i], cos_t, sin_t)
        x2 = _merge(x2, y_pool.reshape(t, POOL_WIDTH), y_hg.reshape(t, HG_W), y_att.reshape(t, ATT_QW), z2,
                    wts["w_br_pool"][i], wts["w_br_hg"][i], wts["w_br_att"][i], wts["w_out"][i])
        x2 = _ffn(x2.reshape(b, n, d), wts["g_ffn"][i], wts["w_up"][i], wts["conv_w"][i], wts["conv_b"][i],
                  wts["w_down"][i]).reshape(t, d)
        x2 = _ple(x2, p[i].reshape(t, PLE_DIM), wts["g_ple"][i], wts["w_ple_gate"][i], wts["w_ple"][i],
                  wts["g_final"], final=(i == depth - 1))
    return x2.reshape(b, n, d)


def kernel(x_prompt, x_sample, p_prompt, p_sample, g_mix, w_in, pool_w, pool_scale, lb_raw_f, lb_raw_b, hg_onorm,
           g_q, g_k, w_br_pool, w_br_hg, w_br_att, w_out, g_ffn, w_up, conv_w, conv_b, w_down, g_ple, w_ple_gate,
           w_ple, g_final):
    depth = w_in.shape[0]

    def vec(a):
        return a.astype(F32).reshape(depth, 1, a.shape[-1])

    wts = dict(
        g_mix=vec(g_mix), w_in=w_in.astype(BF16), pool_w=pool_w.astype(BF16), pool_scale=vec(pool_scale),
        lb_f=_lower_bound(lb_raw_f), lb_b=_lower_bound(lb_raw_b), hg_onorm=hg_onorm, g_q=g_q, g_k=g_k,
        w_br_pool=w_br_pool.astype(BF16), w_br_hg=w_br_hg.astype(BF16), w_br_att=w_br_att.astype(BF16),
        w_out=w_out.astype(BF16), g_ffn=vec(g_ffn), w_up=w_up.astype(BF16), conv_w=conv_w.astype(F32),
        conv_b=vec(conv_b), w_down=w_down.astype(BF16), g_ple=vec(g_ple), w_ple_gate=w_ple_gate.astype(BF16),
        w_ple=w_ple.astype(BF16), g_final=g_final.astype(F32).reshape(1, -1),
    )
    return _trunk(x_prompt, p_prompt, wts), _trunk(x_sample, p_sample, wts)
```

```python
import functools

import jax
import jax.numpy as jnp
from jax import lax
from jax.experimental import pallas as pl
from jax.experimental.pallas import tpu as pltpu

F32 = jnp.float32
BF16 = jnp.bfloat16

EPS = 1e-6
D_MODEL = 1024
GRID_W = 64
PLE_DIM = 256
POOL_WIDTH = 512
POOL_GROUPS = 4
POOL_GDIM = POOL_WIDTH // POOL_GROUPS
POOL_HALF = (1, 2, 4, 8)
POOL_HALO = 8
HALO_BLK = 16
HG_HEADS = 4
HG_DK = 128
HG_DV = 128
HG_W = HG_HEADS * HG_DK
HG_CHUNK = 64
HG_SUB = 16
ATT_HEADS = 16
ATT_KV_HEADS = 4
HEAD_DIM = 64
ATT_QW = ATT_HEADS * HEAD_DIM
ATT_KVW = ATT_KV_HEADS * HEAD_DIM
ROPE_THETA = 10000.0
ROPE_HALF = HEAD_DIM // 2
D_FF = 2816
FFN_HALO = 8
FFN_DOWN_GROUP = 4

OFF_POOL = 0
OFF_HQ = 512
OFF_HFF = 1024
OFF_HFB = 1536
OFF_HI = 2048
OFF_HG = 2560
OFF_AQ = 3072
OFF_AK = 4096
OFF_AV = 4352
OFF_GATE = 4608
IN_WIDTH = 7680
GATE_BLK = 1536

LANES = 128
LOG2E = 1.4426950408889634
QK_SCALE = HEAD_DIM ** -0.5 * LOG2E
SCORE_BOUND_SLACK = 1.02
MAX_BOUNDED_SCORE = 30.0
VMEM_LIMIT = 56 * 1024 * 1024


def _cparams(*sem):
    return pltpu.CompilerParams(dimension_semantics=sem, vmem_limit_bytes=VMEM_LIMIT)


def _tile(n, pref):
    t = min(n, pref)
    while n % t:
        t //= 2
    return t


def _dot(a, b):
    return jnp.dot(a, b, preferred_element_type=F32)


def _dot_nt(a, b):
    return lax.dot_general(a, b, (((1,), (1,)), ((), ())), preferred_element_type=F32)


def _dot_tn(a, b):
    return lax.dot_general(a, b, (((0,), (0,)), ((), ())), preferred_element_type=F32)


def _rms(x, g):
    return x * lax.rsqrt(jnp.mean(x * x, axis=-1, keepdims=True) + EPS) * g


def _norm_matmul_kernel(x_ref, g_ref, w_ref, o_ref, *, tn):
    h = _rms(x_ref[...], g_ref[...]).astype(BF16)
    for j in range(w_ref.shape[1] // tn):
        sl = slice(j * tn, (j + 1) * tn)
        o_ref[:, sl] = _dot(h, w_ref[:, sl]).astype(o_ref.dtype)


def _norm_matmul(x, g, w, out_dtype=BF16, tm_pref=512, tn_pref=512):
    t, k = x.shape
    nout = w.shape[1]
    tm = _tile(t, tm_pref)
    tn = _tile(nout, tn_pref)
    return pl.pallas_call(
        functools.partial(_norm_matmul_kernel, tn=tn),
        grid=(t // tm,),
        in_specs=[
            pl.BlockSpec((tm, k), lambda i: (i, 0)),
            pl.BlockSpec((1, k), lambda i: (0, 0)),
            pl.BlockSpec((k, nout), lambda i: (0, 0)),
        ],
        out_specs=pl.BlockSpec((tm, nout), lambda i: (i, 0)),
        out_shape=jax.ShapeDtypeStruct((t, nout), out_dtype),
        compiler_params=_cparams("parallel"),
        name="norm_in_proj",
    )(x, g, w)


def _pool_kernel(cur_ref, prev_ref, next_ref, w_ref, scale_ref, o_ref, *, n, tn):
    i = pl.program_id(1)
    t0 = i * tn
    u = cur_ref[0].astype(F32)
    hl = POOL_HALO
    prev = jnp.where(i > 0, prev_ref[0, HALO_BLK - hl:].astype(F32), 0.0)
    nxt = jnp.where(t0 + tn < n, next_ref[0, :hl].astype(F32), 0.0)
    e = jnp.concatenate([prev, u, nxt], axis=0)
    s2 = e[0:-1] + e[1:]
    c1 = POOL_GDIM
    s4 = s2[0:-2, c1:] + s2[2:, c1:]
    s8 = s4[0:-4, c1:] + s4[4:, c1:]
    s16 = s8[0:-8, c1:] + s8[8:, c1:]
    sums = (s2[7:7 + tn, :c1], s4[6:6 + tn, :c1], s8[4:4 + tn, :c1], s16[0:tn, :])
    t = t0 + lax.broadcasted_iota(jnp.int32, (tn, 1), 0)
    outs = []
    for g in range(POOL_GROUPS):
        h = POOL_HALF[g]
        cnt = (jnp.minimum(t + h, n) - jnp.maximum(t - h, 0)).astype(F32)
        ug = u[:, g * c1:(g + 1) * c1]
        mixed = (sums[g] / cnt - ug).astype(BF16)
        outs.append(_dot(mixed, w_ref[g]))
    y = jnp.concatenate(outs, axis=1) * scale_ref[...]
    o_ref[0] = y.astype(o_ref.dtype)


def _pool_mixer(z, pool_w, pool_scale, tn_pref=512):
    b, n, _ = z.shape
    tn = _tile(n, tn_pref)
    hb = tn // HALO_BLK
    nhb = n // HALO_BLK
    return pl.pallas_call(
        functools.partial(_pool_kernel, n=n, tn=tn),
        grid=(b, n // tn),
        in_specs=[
            pl.BlockSpec((1, tn, POOL_WIDTH), lambda bi, i: (bi, i, OFF_POOL // POOL_WIDTH)),
            pl.BlockSpec((1, HALO_BLK, POOL_WIDTH),
                         lambda bi, i: (bi, jnp.maximum(i * hb - 1, 0), OFF_POOL // POOL_WIDTH)),
            pl.BlockSpec((1, HALO_BLK, POOL_WIDTH),
                         lambda bi, i: (bi, jnp.minimum((i + 1) * hb, nhb - 1), OFF_POOL // POOL_WIDTH)),
            pl.BlockSpec((POOL_GROUPS, POOL_GDIM, POOL_GDIM), lambda bi, i: (0, 0, 0)),
            pl.BlockSpec((1, POOL_WIDTH), lambda bi, i: (0, 0)),
        ],
        out_specs=pl.BlockSpec((1, tn, POOL_WIDTH), lambda bi, i: (bi, i, 0)),
        out_shape=jax.ShapeDtypeStruct((b, n, POOL_WIDTH), BF16),
        compiler_params=_cparams("parallel", "parallel"),
        name="pool_mixer",
    )(z, z, z, pool_w, pool_scale)


def _split3(x):
    hi = x.astype(BF16)
    r1 = x - hi.astype(F32)
    mid = r1.astype(BF16)
    lo = (r1 - mid.astype(F32)).astype(BF16)
    return hi, mid, lo


def _hgrn_chunk(zq, zf, v, log_lb, log1m_lb, one_m_lb, st_ref, rev):
    c = HG_CHUNK
    zq, zf, v = zq.astype(F32), zf.astype(F32), v.astype(F32)
    q = zq * (HG_DK ** -0.5)
    ez = jnp.exp(-jnp.abs(zf))
    log_sig = jnp.minimum(zf, 0.0) - jnp.log1p(ez)
    k = one_m_lb * (jnp.where(zf >= 0.0, ez, 1.0) / (1.0 + ez))
    ga = log_lb
    gb = log1m_lb + log_sig
    logf = jnp.maximum(ga, gb) + jnp.log1p(jnp.exp(-jnp.abs(ga - gb)))

    row = lax.broadcasted_iota(jnp.int32, (c, c), 0)
    col = lax.broadcasted_iota(jnp.int32, (c, c), 1)
    keep = (col >= row) if rev else (col <= row)
    tri = jnp.where(keep, 1.0, 0.0).astype(BF16)
    hi, mid, lo = _split3(logf)
    cum = (_dot(tri, hi) + _dot(tri, mid) + _dot(tri, lo)) * LOG2E
    last = cum[0:1] if rev else cum[c - 1:c]
    qhat = (q * jnp.exp2(cum)).astype(BF16)
    khat = (k * jnp.exp2(last - cum)).astype(BF16)
    dec = jnp.exp2(last)

    nsub = c // HG_SUB
    half = HG_SUB // 2
    heads = [slice(h * HG_DK, (h + 1) * HG_DK) for h in range(HG_HEADS)]
    a_rows = [[] for _ in range(HG_HEADS)]
    col_h = lax.broadcasted_iota(jnp.int32, (half, c), 1)
    for jb in range(nsub):
        r0 = jb * HG_SUB
        lo_r, hi_r = (r0 + HG_SUB, c) if rev else (0, r0)
        if hi_r > lo_r:
            ridx = lo_r if rev else hi_r - 1
            ref_row = cum[ridx:ridx + 1]
            qt = (q[r0:r0 + HG_SUB] * jnp.exp2(cum[r0:r0 + HG_SUB] - ref_row)).astype(BF16)
            pieces = [jnp.zeros((lo_r, HG_W), BF16),
                      (k[lo_r:hi_r] * jnp.exp2(ref_row - cum[lo_r:hi_r])).astype(BF16),
                      jnp.zeros((c - hi_r, HG_W), BF16)]
            kt = jnp.concatenate([p for p in pieces if p.shape[0]], axis=0)
            offs = [_dot_nt(qt[:, sl], kt[:, sl]) for sl in heads]
            halves = [[o[:half] for o in offs], [o[half:] for o in offs]]
        else:
            halves = [[jnp.zeros((half, c), F32) for _ in heads] for _ in range(2)]
        for l in range(HG_SUB):
            lr = r0 + l
            for which in range(2):
                if (which == 0 and not rev and l >= half) or (which == 1 and rev and l < half):
                    continue
                rr = r0 + which * half
                x = q[rr:rr + half] * jnp.exp2(jnp.minimum(cum[rr:rr + half] - cum[lr:lr + 1], 0.0)) * k[lr:lr + 1]
                for h, sl in enumerate(heads):
                    s = jnp.sum(x[:, sl], axis=-1, keepdims=True)
                    halves[which][h] = jnp.where(col_h == lr, s, halves[which][h])
        for h in range(HG_HEADS):
            a_rows[h] += [halves[0][h], halves[1][h]]

    outs = []
    for h, sl in enumerate(heads):
        a = jnp.where(keep, jnp.concatenate(a_rows[h], axis=0), 0.0).astype(BF16)
        vh = v[:, sl].astype(BF16)
        st = st_ref[h]
        outs.append(_dot(a, vh) + _dot_nt(qhat[:, sl], st.astype(BF16)))
        st_ref[h] = st * dec[:, sl] + _dot_tn(vh, khat[:, sl])
    return jnp.concatenate(outs, axis=1)


def _hgrn_kernel(*refs, rev, combine, chunks):
    if combine:
        zq_ref, zf_ref, zi_ref, zg_ref, ofw_ref, llb_ref, l1m_ref, oml_ref, gon_ref, o_ref, st_ref = refs
    else:
        zq_ref, zf_ref, zi_ref, llb_ref, l1m_ref, oml_ref, o_ref, st_ref = refs

    @pl.when(pl.program_id(1) == 0)
    def _():
        st_ref[...] = jnp.zeros_like(st_ref)

    order = range(chunks - 1, -1, -1) if rev else range(chunks)
    for ci in order:
        rows = slice(ci * HG_CHUNK, (ci + 1) * HG_CHUNK)
        o = _hgrn_chunk(zq_ref[0, rows], zf_ref[0, rows], zi_ref[0, rows], llb_ref[...], l1m_ref[...],
                        oml_ref[...], st_ref, rev)
        if combine:
            o = o + ofw_ref[0, rows]
            zg = zg_ref[0, rows].astype(F32)
            y = jnp.concatenate([_rms(o[:, h * HG_DV:(h + 1) * HG_DV], gon_ref[...]) for h in range(HG_HEADS)],
                                axis=1)
            o = y * (zg * jax.nn.sigmoid(zg))
        o_ref[0, rows] = o.astype(o_ref.dtype)


def _hgrn2_mixer(z, lb_f, lb_b, g_onorm, chunks_pref=4):
    b, n, _ = z.shape
    chunks = chunks_pref if (n // HG_CHUNK) % chunks_pref == 0 else 1
    rows_blk = HG_CHUNK * chunks
    nblk = n // rows_blk

    def lb_rows(lbv):
        lbv = lbv.reshape(1, HG_W).astype(F32)
        return jnp.log(lbv), jnp.log1p(-lbv), 1.0 - lbv

    def spec(col, rev):
        if rev:
            return pl.BlockSpec((1, rows_blk, HG_W), lambda bi, i: (bi, nblk - 1 - i, col))
        return pl.BlockSpec((1, rows_blk, HG_W), lambda bi, i: (bi, i, col))

    vec = pl.BlockSpec((1, HG_W), lambda bi, i: (0, 0))
    scratch = [pltpu.VMEM((HG_HEADS, HG_DV, HG_DK), F32)]
    o_fw = pl.pallas_call(
        functools.partial(_hgrn_kernel, rev=False, combine=False, chunks=chunks),
        grid=(b, nblk),
        in_specs=[spec(OFF_HQ // HG_W, False), spec(OFF_HFF // HG_W, False), spec(OFF_HI // HG_W, False),
                  vec, vec, vec],
        out_specs=spec(0, False),
        out_shape=jax.ShapeDtypeStruct((b, n, HG_W), F32),
        scratch_shapes=scratch,
        compiler_params=_cparams("parallel", "arbitrary"),
        name="hgrn2_forward",
    )(z, z, z, *lb_rows(lb_f))
    return pl.pallas_call(
        functools.partial(_hgrn_kernel, rev=True, combine=True, chunks=chunks),
        grid=(b, nblk),
        in_specs=[spec(OFF_HQ // HG_W, True), spec(OFF_HFB // HG_W, True), spec(OFF_HI // HG_W, True),
                  spec(OFF_HG // HG_W, True), spec(0, True), vec, vec, vec,
                  pl.BlockSpec((1, HG_DV), lambda bi, i: (0, 0))],
        out_specs=spec(0, True),
        out_shape=jax.ShapeDtypeStruct((b, n, HG_W), BF16),
        scratch_shapes=scratch,
        compiler_params=_cparams("parallel", "arbitrary"),
        name="hgrn2_backward_combine",
    )(z, z, z, z, o_fw, *lb_rows(lb_b), g_onorm.reshape(1, HG_DV).astype(F32))


def _headnorm_rope(x, avg, g, cosv, sinv):
    hi = (x * x).astype(BF16)
    lo = (x * x - hi.astype(F32)).astype(BF16)
    ms = _dot(hi, avg) + _dot(lo, avg)
    y = x * lax.rsqrt(ms + EPS) * g
    lane = lax.broadcasted_iota(jnp.int32, y.shape, 1)
    first = (lane % ROPE_HALF) < (ROPE_HALF // 2)
    partner = jnp.where(first, pltpu.roll(y, LANES - ROPE_HALF // 2, 1), pltpu.roll(y, ROPE_HALF // 2, 1))
    return y * cosv + partner * sinv


def _attn_prep_kernel(zq_ref, zk_ref, zv_ref, cos_ref, sin_ref, gq_ref, gk_ref, avg_ref, qt_ref, k_ref, vt_ref):
    cosv = cos_ref[...]
    sinv = sin_ref[...]
    avg = avg_ref[...]
    for p in range(ATT_QW // LANES):
        sl = slice(p * LANES, (p + 1) * LANES)
        y = _headnorm_rope(zq_ref[0, :, sl].astype(F32), avg, gq_ref[...], cosv, sinv) * QK_SCALE
        qt_ref[0, sl, :] = y.T.astype(qt_ref.dtype)
    for p in range(ATT_KVW // LANES):
        sl = slice(p * LANES, (p + 1) * LANES)
        kn = _headnorm_rope(zk_ref[0, :, sl].astype(F32), avg, gk_ref[...], cosv, sinv)
        k_ref[0, :, sl] = kn.astype(k_ref.dtype)
        vt_ref[0, sl, :] = zv_ref[0, :, sl].astype(F32).T.astype(vt_ref.dtype)


def _flash_kernel(bound_ref, fast_ref, qt_ref, k_ref, vt_ref, o_ref, qpad_ref, m_ref, l_ref, acc_ref):
    kv = pl.program_id(2)
    tq = qt_ref.shape[2]
    hd = HEAD_DIM
    group = ATT_HEADS // ATT_KV_HEADS

    @pl.when(kv == 0)
    def _():
        zeros = jnp.zeros((hd, tq), qpad_ref.dtype)
        for h in range(ATT_HEADS):
            qh = qt_ref[0, h * hd:(h + 1) * hd, :]
            if (h // group) % 2 == 0:
                qpad_ref[h] = jnp.concatenate([qh, zeros], axis=0)
            else:
                qpad_ref[h] = jnp.concatenate([zeros, qh], axis=0)
        m_ref[...] = jnp.full_like(m_ref, -jnp.inf)
        l_ref[...] = jnp.zeros_like(l_ref)
        acc_ref[...] = jnp.zeros_like(acc_ref)

    def scores(h):
        a = h // (2 * group)
        return _dot(k_ref[0, :, a * LANES:(a + 1) * LANES], qpad_ref[h])

    def all_heads(update):
        st_next = scores(0)
        for h in range(ATT_HEADS):
            st = st_next
            if h + 1 < ATT_HEADS:
                st_next = scores(h + 1)
            g = h // group
            update(h, st, vt_ref[0, g * hd:(g + 1) * hd, :], slice(h * hd, (h + 1) * hd))

    def update_bounded(h, st, vt, rows):
        pt = jnp.exp2(st - bound_ref[0])
        l_ref[h] = l_ref[h] + jnp.sum(pt, axis=0, keepdims=True)
        acc_ref[rows] = acc_ref[rows] + _dot(vt, pt.astype(BF16))

    def update_running_max(h, st, vt, rows):
        m_prev = m_ref[h]
        m_new = jnp.maximum(m_prev, jnp.max(st, axis=0, keepdims=True))
        alpha = jnp.exp2(m_prev - m_new)
        pt = jnp.exp2(st - m_new)
        l_ref[h] = alpha * l_ref[h] + jnp.sum(pt, axis=0, keepdims=True)
        m_ref[h] = m_new
        acc_ref[rows] = acc_ref[rows] * alpha + _dot(vt, pt.astype(BF16))

    @pl.when(fast_ref[0] == 1)
    def _():
        all_heads(update_bounded)

    @pl.when(fast_ref[0] != 1)
    def _():
        all_heads(update_running_max)

    @pl.when(kv == pl.num_programs(2) - 1)
    def _():
        for h in range(ATT_HEADS):
            rows = slice(h * hd, (h + 1) * hd)
            acc_ref[rows] = acc_ref[rows] * (1.0 / l_ref[h])
        for p in range(ATT_QW // LANES):
            sl = slice(p * LANES, (p + 1) * LANES)
            o_ref[0, :, sl] = acc_ref[sl].T.astype(o_ref.dtype)


def _rope_tables(n):
    rows = n // GRID_W
    row = jnp.repeat(jnp.arange(rows, dtype=F32), GRID_W)
    colp = jnp.tile(jnp.arange(GRID_W, dtype=F32), rows)
    inv_freq = 1.0 / (ROPE_THETA ** (jnp.arange(0, ROPE_HALF, 2, dtype=F32) / ROPE_HALF))
    ang_r = row[:, None] * inv_freq
    ang_c = colp[:, None] * inv_freq
    cos64 = jnp.concatenate([jnp.cos(ang_r), jnp.cos(ang_r), jnp.cos(ang_c), jnp.cos(ang_c)], axis=-1)
    sin64 = jnp.concatenate([-jnp.sin(ang_r), jnp.sin(ang_r), -jnp.sin(ang_c), jnp.sin(ang_c)], axis=-1)
    return jnp.tile(cos64, (1, 2)), jnp.tile(sin64, (1, 2))


def _gqa_axial(z, g_q, g_k, cos_t, sin_t, tm_pref=512, tq_pref=1024, tk_pref=1024):
    b, n, _ = z.shape
    tm = _tile(n, tm_pref)
    seg = lax.broadcasted_iota(jnp.int32, (LANES, LANES), 0) // HEAD_DIM
    seg_t = lax.broadcasted_iota(jnp.int32, (LANES, LANES), 1) // HEAD_DIM
    avg = jnp.where(seg == seg_t, 1.0 / HEAD_DIM, 0.0).astype(BF16)
    gq2 = jnp.tile(g_q.reshape(1, HEAD_DIM).astype(F32), (1, 2))
    gk2 = jnp.tile(g_k.reshape(1, HEAD_DIM).astype(F32), (1, 2))
    vec = pl.BlockSpec((1, LANES), lambda bi, i: (0, 0))
    tab = pl.BlockSpec((tm, LANES), lambda bi, i: (i, 0))
    qt, kh, vt = pl.pallas_call(
        _attn_prep_kernel,
        grid=(b, n // tm),
        in_specs=[
            pl.BlockSpec((1, tm, ATT_QW), lambda bi, i: (bi, i, OFF_AQ // ATT_QW)),
            pl.BlockSpec((1, tm, ATT_KVW), lambda bi, i: (bi, i, OFF_AK // ATT_KVW)),
            pl.BlockSpec((1, tm, ATT_KVW), lambda bi, i: (bi, i, OFF_AV // ATT_KVW)),
            tab, tab, vec, vec,
            pl.BlockSpec((LANES, LANES), lambda bi, i: (0, 0)),
        ],
        out_specs=[
            pl.BlockSpec((1, ATT_QW, tm), lambda bi, i: (bi, 0, i)),
            pl.BlockSpec((1, tm, ATT_KVW), lambda bi, i: (bi, i, 0)),
            pl.BlockSpec((1, ATT_KVW, tm), lambda bi, i: (bi, 0, i)),
        ],
        out_shape=[
            jax.ShapeDtypeStruct((b, ATT_QW, n), BF16),
            jax.ShapeDtypeStruct((b, n, ATT_KVW), BF16),
            jax.ShapeDtypeStruct((b, ATT_KVW, n), BF16),
        ],
        compiler_params=_cparams("parallel", "parallel"),
        name="attn_prep",
    )(z, z, z, cos_t, sin_t, gq2, gk2, avg)

    bound = (SCORE_BOUND_SLACK * HEAD_DIM * QK_SCALE) * jnp.max(jnp.abs(g_q)) * jnp.max(jnp.abs(g_k))
    bound = bound.astype(F32).reshape(1)
    fast = (bound <= MAX_BOUNDED_SCORE).astype(jnp.int32)

    tq = _tile(n, tq_pref)
    tk = _tile(n, tk_pref)
    return pl.pallas_call(
        _flash_kernel,
        grid_spec=pltpu.PrefetchScalarGridSpec(
            num_scalar_prefetch=2,
            grid=(b, n // tq, n // tk),
            in_specs=[
                pl.BlockSpec((1, ATT_QW, tq), lambda bi, i, j, *_: (bi, 0, i)),
                pl.BlockSpec((1, tk, ATT_KVW), lambda bi, i, j, *_: (bi, j, 0)),
                pl.BlockSpec((1, ATT_KVW, tk), lambda bi, i, j, *_: (bi, 0, j)),
            ],
            out_specs=pl.BlockSpec((1, tq, ATT_QW), lambda bi, i, j, *_: (bi, i, 0)),
            scratch_shapes=[
                pltpu.VMEM((ATT_HEADS, LANES, tq), BF16),
                pltpu.VMEM((ATT_HEADS, 1, tq), F32),
                pltpu.VMEM((ATT_HEADS, 1, tq), F32),
                pltpu.VMEM((ATT_QW, tq), F32),
            ],
        ),
        out_shape=jax.ShapeDtypeStruct((b, n, ATT_QW), BF16),
        compiler_params=_cparams("parallel", "parallel", "arbitrary"),
        name="flash_gqa",
    )(bound, fast, qt, kh, vt)


def _merge_kernel(x_ref, yp_ref, yh_ref, ya_ref, ga_ref, gb_ref, wp_ref, wh_ref, wa_ref, wo_ref, o_ref):
    ga = jax.nn.sigmoid(ga_ref[...].astype(F32))
    gb = jax.nn.sigmoid(gb_ref[...].astype(F32))
    d = D_MODEL
    g_pool = ga[:, :d]
    g_hg = jnp.concatenate([ga[:, d:], gb[:, :2 * d - GATE_BLK]], axis=1)
    g_att = gb[:, 2 * d - GATE_BLK:]
    merged = (g_pool * _dot(yp_ref[...], wp_ref[...])
              + g_hg * _dot(yh_ref[...], wh_ref[...])
              + g_att * _dot(ya_ref[...], wa_ref[...]))
    o_ref[...] = x_ref[...] + _dot(merged.astype(BF16), wo_ref[...])


def _merge(x, y_pool, y_hg, y_att, z2, w_p, w_h, w_a, w_o, tm_pref=512):
    t, d = x.shape
    tm = _tile(t, tm_pref)

    def rowblk(w):
        return pl.BlockSpec((tm, w), lambda i: (i, 0))

    def whole(w):
        return pl.BlockSpec(w.shape, lambda i: (0, 0))

    return pl.pallas_call(
        _merge_kernel,
        grid=(t // tm,),
        in_specs=[
            rowblk(d), rowblk(POOL_WIDTH), rowblk(HG_W), rowblk(ATT_QW),
            pl.BlockSpec((tm, GATE_BLK), lambda i: (i, OFF_GATE // GATE_BLK)),
            pl.BlockSpec((tm, GATE_BLK), lambda i: (i, OFF_GATE // GATE_BLK + 1)),
            whole(w_p), whole(w_h), whole(w_a), whole(w_o),
        ],
        out_specs=rowblk(d),
        out_shape=jax.ShapeDtypeStruct((t, d), F32),
        compiler_params=_cparams("parallel"),
        name="merge_out_proj",
    )(x, y_pool, y_hg, y_att, z2, z2, w_p, w_h, w_a, w_o)


def _gelu_tanh(a):
    half = 0.5 * a
    inner = a * (0.7978845608028654 + (0.7978845608028654 * 0.044715) * (a * a))
    return half + half * jnp.tanh(inner)


def _ffn_kernel(x_ref, prev_ref, next_ref, g_ref, wu_ref, cw_ref, cb_ref, wd_ref, o_ref, h_ref, *, n, tm, tf):
    i = pl.program_id(1)
    hl = FFN_HALO
    g = g_ref[...]
    h_ref[hl:hl + tm] = _rms(x_ref[0], g).astype(BF16)
    h_ref[0:hl] = jnp.where(i > 0, _rms(prev_ref[0], g), 0.0).astype(BF16)
    h_ref[hl + tm:] = jnp.where((i + 1) * tm < n, _rms(next_ref[0], g), 0.0).astype(BF16)
    h = h_ref[...]
    nf = D_FF // tf

    def up(j):
        return tuple(_dot(h, wu_ref[:, c0 + j * tf:c0 + (j + 1) * tf]) for c0 in (0, D_FF))

    rows = tm + 2 * hl

    def conv(u, c0):
        cw = cw_ref[:, c0:c0 + tf]
        prev = pltpu.roll(u, 1, 0)[hl:hl + tm]
        nxt_ = pltpu.roll(u, rows - 1, 0)[hl:hl + tm]
        return prev * cw[0:1] + u[hl:hl + tm] * cw[1:2] + nxt_ * cw[2:3] + cb_ref[:, c0:c0 + tf]

    acc = None
    acts = []
    nxt = up(0)
    for j in range(nf):
        ua, ug = nxt
        if j + 1 < nf:
            nxt = up(j + 1)
        acts.append((_gelu_tanh(conv(ua, j * tf)) * conv(ug, D_FF + j * tf)).astype(BF16))
        if len(acts) == FFN_DOWN_GROUP or j == nf - 1:
            k0 = (j + 1 - len(acts)) * tf
            part = _dot(jnp.concatenate(acts, axis=1), wd_ref[k0:(j + 1) * tf, :])
            acc = part if acc is None else acc + part
            acts = []
    o_ref[0] = x_ref[0] + acc


def _ffn(x, g, w_up, conv_w, conv_b, w_down, tm_pref=512, tf_pref=256):
    b, n, d = x.shape
    tm = _tile(n, tm_pref)
    tf = _tile(D_FF, tf_pref)
    hb = tm // FFN_HALO
    nhb = n // FFN_HALO

    def whole(a):
        return pl.BlockSpec(a.shape, lambda bi, i: (0,) * a.ndim)

    return pl.pallas_call(
        functools.partial(_ffn_kernel, n=n, tm=tm, tf=tf),
        grid=(b, n // tm),
        in_specs=[
            pl.BlockSpec((1, tm, d), lambda bi, i: (bi, i, 0)),
            pl.BlockSpec((1, FFN_HALO, d), lambda bi, i: (bi, jnp.maximum(i * hb - 1, 0), 0)),
            pl.BlockSpec((1, FFN_HALO, d), lambda bi, i: (bi, jnp.minimum((i + 1) * hb, nhb - 1), 0)),
            whole(g), whole(w_up), whole(conv_w), whole(conv_b), whole(w_down),
        ],
        out_specs=pl.BlockSpec((1, tm, d), lambda bi, i: (bi, i, 0)),
        out_shape=jax.ShapeDtypeStruct((b, n, d), F32),
        scratch_shapes=[pltpu.VMEM((tm + 2 * FFN_HALO, d), BF16)],
        compiler_params=_cparams("parallel", "parallel"),
        name="conv_glu_ffn",
    )(x, x, x, g, w_up, conv_w, conv_b, w_down)


def _ple_kernel(x_ref, p_ref, g_ref, wg_ref, wp_ref, gf_ref, o_ref, *, final):
    x = x_ref[...]
    gate = jax.nn.sigmoid(_dot(_rms(x, g_ref[...]).astype(BF16), wg_ref[...]))
    y = x + gate * _dot(p_ref[0].astype(BF16), wp_ref[...])
    if final:
        y = _rms(y, gf_ref[...])
    o_ref[...] = y


def _ple(x, p, layer, g, w_gate, w_ple, g_final, final, tm_pref=512):
    t, d = x.shape
    tm = _tile(t, tm_pref)
    return pl.pallas_call(
        functools.partial(_ple_kernel, final=final),
        grid=(t // tm,),
        in_specs=[
            pl.BlockSpec((tm, d), lambda i: (i, 0)),
            pl.BlockSpec((1, tm, PLE_DIM), lambda i: (layer, i, 0)),
            pl.BlockSpec((1, d), lambda i: (0, 0)),
            pl.BlockSpec((d, d), lambda i: (0, 0)),
            pl.BlockSpec((PLE_DIM, d), lambda i: (0, 0)),
            pl.BlockSpec((1, d), lambda i: (0, 0)),
        ],
        out_specs=pl.BlockSpec((tm, d), lambda i: (i, 0)),
        out_shape=jax.ShapeDtypeStruct((t, d), F32),
        compiler_params=_cparams("parallel"),
        name="ple_final_norm" if final else "ple",
    )(x, p, g, w_gate, w_ple, g_final)


def _lower_bound(lb_raw):
    s = jnp.cumsum(jax.nn.softmax(lb_raw.astype(F32), axis=0), axis=0)
    return s - s[0:1]


def _trunk(x, p, wts):
    b, n, d = x.shape
    t = b * n
    depth = wts["w_in"].shape[0]
    cos_t, sin_t = _rope_tables(n)
    x2 = x.reshape(t, d)
    for i in range(depth):
        z2 = _norm_matmul(x2, wts["g_mix"][i], wts["w_in"][i])
        z = z2.reshape(b, n, IN_WIDTH)
        y_pool = _pool_mixer(z, wts["pool_w"][i], wts["pool_scale"][i])
        y_hg = _hgrn2_mixer(z, wts["lb_f"][i], wts["lb_b"][i], wts["hg_onorm"][i])
        y_att = _gqa_axial(z, wts["g_q"][i], wts["g_k"][i], cos_t, sin_t)
        x2 = _merge(x2, y_pool.reshape(t, POOL_WIDTH), y_hg.reshape(t, HG_W), y_att.reshape(t, ATT_QW), z2,
                    wts["w_br_pool"][i], wts["w_br_hg"][i], wts["w_br_att"][i], wts["w_out"][i])
        x2 = _ffn(x2.reshape(b, n, d), wts["g_ffn"][i], wts["w_up"][i], wts["conv_w"][i], wts["conv_b"][i],
                  wts["w_down"][i]).reshape(t, d)
        x2 = _ple(x2, p.reshape(depth, t, PLE_DIM), i, wts["g_ple"][i], wts["w_ple_gate"][i], wts["w_ple"][i],
                  wts["g_final"], final=(i == depth - 1))
    return x2.reshape(b, n, d)


def kernel(x_prompt, x_sample, p_prompt, p_sample, g_mix, w_in, pool_w, pool_scale, lb_raw_f, lb_raw_b, hg_onorm,
           g_q, g_k, w_br_pool, w_br_hg, w_br_att, w_out, g_ffn, w_up, conv_w, conv_b, w_down, g_ple, w_ple_gate,
           w_ple, g_final):
    depth = w_in.shape[0]

    def vec(a):
        return a.astype(F32).reshape(depth, 1, a.shape[-1])

    wts = dict(
        g_mix=vec(g_mix), w_in=w_in.astype(BF16), pool_w=pool_w.astype(BF16), pool_scale=vec(pool_scale),
        lb_f=_lower_bound(lb_raw_f), lb_b=_lower_bound(lb_raw_b), hg_onorm=hg_onorm, g_q=g_q, g_k=g_k,
        w_br_pool=w_br_pool.astype(BF16), w_br_hg=w_br_hg.astype(BF16), w_br_att=w_br_att.astype(BF16),
        w_out=w_out.astype(BF16), g_ffn=vec(g_ffn), w_up=w_up.astype(BF16), conv_w=conv_w.astype(F32),
        conv_b=vec(conv_b), w_down=w_down.astype(BF16), g_ple=vec(g_ple), w_ple_gate=w_ple_gate.astype(BF16),
        w_ple=w_ple.astype(BF16), g_final=g_final.astype(F32).reshape(1, -1),
    )
    return _trunk(x_prompt, p_prompt, wts), _trunk(x_sample, p_sample, wts)
```

```python
import functools

import jax
import jax.numpy as jnp
from jax import lax
from jax.experimental import pallas as pl
from jax.experimental.pallas import tpu as pltpu

F32 = jnp.float32
BF16 = jnp.bfloat16

EPS = 1e-6
D_MODEL = 1024
GRID_W = 64
PLE_DIM = 256
POOL_WIDTH = 512
POOL_GROUPS = 4
POOL_GDIM = POOL_WIDTH // POOL_GROUPS
POOL_HALF = (1, 2, 4, 8)
POOL_HALO = 8
HALO_BLK = 16
HG_HEADS = 4
HG_DK = 128
HG_DV = 128
HG_W = HG_HEADS * HG_DK
HG_CHUNK = 64
HG_SUB = 16
ATT_HEADS = 16
ATT_KV_HEADS = 4
HEAD_DIM = 64
ATT_QW = ATT_HEADS * HEAD_DIM
ATT_KVW = ATT_KV_HEADS * HEAD_DIM
ROPE_THETA = 10000.0
ROPE_HALF = HEAD_DIM // 2
D_FF = 2816
FFN_HALO = 8
FFN_DOWN_GROUP = 4

OFF_POOL = 0
OFF_HQ = 512
OFF_HFF = 1024
OFF_HFB = 1536
OFF_HI = 2048
OFF_HG = 2560
OFF_AQ = 3072
OFF_AK = 4096
OFF_AV = 4352
OFF_GATE = 4608
IN_WIDTH = 7680
GATE_BLK = 1536

LANES = 128
LOG2E = 1.4426950408889634
QK_SCALE = HEAD_DIM ** -0.5 * LOG2E
SCORE_BOUND_SLACK = 1.02
MAX_BOUNDED_SCORE = 30.0
VMEM_LIMIT = 56 * 1024 * 1024


def _cparams(*sem):
    return pltpu.CompilerParams(dimension_semantics=sem, vmem_limit_bytes=VMEM_LIMIT)


def _tile(n, pref):
    t = min(n, pref)
    while n % t:
        t //= 2
    return t


def _dot(a, b):
    return jnp.dot(a, b, preferred_element_type=F32)


def _dot_nt(a, b):
    return lax.dot_general(a, b, (((1,), (1,)), ((), ())), preferred_element_type=F32)


def _dot_tn(a, b):
    return lax.dot_general(a, b, (((0,), (0,)), ((), ())), preferred_element_type=F32)


def _rms(x, g):
    return x * lax.rsqrt(jnp.mean(x * x, axis=-1, keepdims=True) + EPS) * g


def _norm_matmul_kernel(x_ref, g_ref, w_ref, o_ref, *, tn):
    h = _rms(x_ref[...], g_ref[...]).astype(BF16)
    for j in range(w_ref.shape[1] // tn):
        sl = slice(j * tn, (j + 1) * tn)
        o_ref[:, sl] = _dot(h, w_ref[:, sl]).astype(o_ref.dtype)


def _norm_matmul(x, g, w, out_dtype=BF16, tm_pref=512, tn_pref=512):
    t, k = x.shape
    nout = w.shape[1]
    tm = _tile(t, tm_pref)
    tn = _tile(nout, tn_pref)
    return pl.pallas_call(
        functools.partial(_norm_matmul_kernel, tn=tn),
        grid=(t // tm,),
        in_specs=[
            pl.BlockSpec((tm, k), lambda i: (i, 0)),
            pl.BlockSpec((1, k), lambda i: (0, 0)),
            pl.BlockSpec((k, nout), lambda i: (0, 0)),
        ],
        out_specs=pl.BlockSpec((tm, nout), lambda i: (i, 0)),
        out_shape=jax.ShapeDtypeStruct((t, nout), out_dtype),
        compiler_params=_cparams("parallel"),
        name="norm_in_proj",
    )(x, g, w)


def _pool_kernel(cur_ref, prev_ref, next_ref, w_ref, scale_ref, o_ref, *, n, tn):
    i = pl.program_id(1)
    t0 = i * tn
    u = cur_ref[0].astype(F32)
    hl = POOL_HALO
    prev = jnp.where(i > 0, prev_ref[0, HALO_BLK - hl:].astype(F32), 0.0)
    nxt = jnp.where(t0 + tn < n, next_ref[0, :hl].astype(F32), 0.0)
    e = jnp.concatenate([prev, u, nxt], axis=0)
    s2 = e[0:-1] + e[1:]
    c1 = POOL_GDIM
    s4 = s2[0:-2, c1:] + s2[2:, c1:]
    s8 = s4[0:-4, c1:] + s4[4:, c1:]
    s16 = s8[0:-8, c1:] + s8[8:, c1:]
    sums = (s2[7:7 + tn, :c1], s4[6:6 + tn, :c1], s8[4:4 + tn, :c1], s16[0:tn, :])
    t = t0 + lax.broadcasted_iota(jnp.int32, (tn, 1), 0)
    outs = []
    for g in range(POOL_GROUPS):
        h = POOL_HALF[g]
        cnt = (jnp.minimum(t + h, n) - jnp.maximum(t - h, 0)).astype(F32)
        ug = u[:, g * c1:(g + 1) * c1]
        mixed = (sums[g] / cnt - ug).astype(BF16)
        outs.append(_dot(mixed, w_ref[g]))
    y = jnp.concatenate(outs, axis=1) * scale_ref[...]
    o_ref[0] = y.astype(o_ref.dtype)


def _pool_mixer(z, pool_w, pool_scale, tn_pref=512):
    b, n, _ = z.shape
    tn = _tile(n, tn_pref)
    hb = tn // HALO_BLK
    nhb = n // HALO_BLK
    return pl.pallas_call(
        functools.partial(_pool_kernel, n=n, tn=tn),
        grid=(b, n // tn),
        in_specs=[
            pl.BlockSpec((1, tn, POOL_WIDTH), lambda bi, i: (bi, i, OFF_POOL // POOL_WIDTH)),
            pl.BlockSpec((1, HALO_BLK, POOL_WIDTH),
                         lambda bi, i: (bi, jnp.maximum(i * hb - 1, 0), OFF_POOL // POOL_WIDTH)),
            pl.BlockSpec((1, HALO_BLK, POOL_WIDTH),
                         lambda bi, i: (bi, jnp.minimum((i + 1) * hb, nhb - 1), OFF_POOL // POOL_WIDTH)),
            pl.BlockSpec((POOL_GROUPS, POOL_GDIM, POOL_GDIM), lambda bi, i: (0, 0, 0)),
            pl.BlockSpec((1, POOL_WIDTH), lambda bi, i: (0, 0)),
        ],
        out_specs=pl.BlockSpec((1, tn, POOL_WIDTH), lambda bi, i: (bi, i, 0)),
        out_shape=jax.ShapeDtypeStruct((b, n, POOL_WIDTH), BF16),
        compiler_params=_cparams("parallel", "parallel"),
        name="pool_mixer",
    )(z, z, z, pool_w, pool_scale)


def _split3(x):
    hi = x.astype(BF16)
    r1 = x - hi.astype(F32)
    mid = r1.astype(BF16)
    lo = (r1 - mid.astype(F32)).astype(BF16)
    return hi, mid, lo


def _hgrn_chunk(zq, zf, v, log_lb, log1m_lb, one_m_lb, st_ref, rev):
    c = HG_CHUNK
    zq, zf, v = zq.astype(F32), zf.astype(F32), v.astype(F32)
    q = zq * (HG_DK ** -0.5)
    ez = jnp.exp(-jnp.abs(zf))
    log_sig = jnp.minimum(zf, 0.0) - jnp.log1p(ez)
    k = one_m_lb * (jnp.where(zf >= 0.0, ez, 1.0) / (1.0 + ez))
    ga = log_lb
    gb = log1m_lb + log_sig
    logf = jnp.maximum(ga, gb) + jnp.log1p(jnp.exp(-jnp.abs(ga - gb)))

    row = lax.broadcasted_iota(jnp.int32, (c, c), 0)
    col = lax.broadcasted_iota(jnp.int32, (c, c), 1)
    keep = (col >= row) if rev else (col <= row)
    tri = jnp.where(keep, 1.0, 0.0).astype(BF16)
    hi, mid, lo = _split3(logf)
    cum = (_dot(tri, hi) + _dot(tri, mid) + _dot(tri, lo)) * LOG2E
    last = cum[0:1] if rev else cum[c - 1:c]
    qhat = (q * jnp.exp2(cum)).astype(BF16)
    khat = (k * jnp.exp2(last - cum)).astype(BF16)
    dec = jnp.exp2(last)

    nsub = c // HG_SUB
    half = HG_SUB // 2
    heads = [slice(h * HG_DK, (h + 1) * HG_DK) for h in range(HG_HEADS)]
    a_rows = [[] for _ in range(HG_HEADS)]
    col_h = lax.broadcasted_iota(jnp.int32, (half, c), 1)
    for jb in range(nsub):
        r0 = jb * HG_SUB
        lo_r, hi_r = (r0 + HG_SUB, c) if rev else (0, r0)
        if hi_r > lo_r:
            ridx = lo_r if rev else hi_r - 1
            ref_row = cum[ridx:ridx + 1]
            qt = (q[r0:r0 + HG_SUB] * jnp.exp2(cum[r0:r0 + HG_SUB] - ref_row)).astype(BF16)
            pieces = [jnp.zeros((lo_r, HG_W), BF16),
                      (k[lo_r:hi_r] * jnp.exp2(ref_row - cum[lo_r:hi_r])).astype(BF16),
                      jnp.zeros((c - hi_r, HG_W), BF16)]
            kt = jnp.concatenate([p for p in pieces if p.shape[0]], axis=0)
            offs = [_dot_nt(qt[:, sl], kt[:, sl]) for sl in heads]
            halves = [[o[:half] for o in offs], [o[half:] for o in offs]]
        else:
            halves = [[jnp.zeros((half, c), F32) for _ in heads] for _ in range(2)]
        for l in range(HG_SUB):
            lr = r0 + l
            for which in range(2):
                if (which == 0 and not rev and l >= half) or (which == 1 and rev and l < half):
                    continue
                rr = r0 + which * half
                x = q[rr:rr + half] * jnp.exp2(jnp.minimum(cum[rr:rr + half] - cum[lr:lr + 1], 0.0)) * k[lr:lr + 1]
                for h, sl in enumerate(heads):
                    s = jnp.sum(x[:, sl], axis=-1, keepdims=True)
                    halves[which][h] = jnp.where(col_h == lr, s, halves[which][h])
        for h in range(HG_HEADS):
            a_rows[h] += [halves[0][h], halves[1][h]]

    outs = []
    for h, sl in enumerate(heads):
        a = jnp.where(keep, jnp.concatenate(a_rows[h], axis=0), 0.0).astype(BF16)
        vh = v[:, sl].astype(BF16)
        st = st_ref[h]
        outs.append(_dot(a, vh) + _dot_nt(qhat[:, sl], st.astype(BF16)))
        st_ref[h] = st * dec[:, sl] + _dot_tn(vh, khat[:, sl])
    return jnp.concatenate(outs, axis=1)


def _hgrn_kernel(*refs, rev, combine, chunks):
    if combine:
        zq_ref, zf_ref, zi_ref, zg_ref, ofw_ref, llb_ref, l1m_ref, oml_ref, gon_ref, o_ref, st_ref = refs
    else:
        zq_ref, zf_ref, zi_ref, llb_ref, l1m_ref, oml_ref, o_ref, st_ref = refs

    @pl.when(pl.program_id(1) == 0)
    def _():
        st_ref[...] = jnp.zeros_like(st_ref)

    order = range(chunks - 1, -1, -1) if rev else range(chunks)
    for ci in order:
        rows = slice(ci * HG_CHUNK, (ci + 1) * HG_CHUNK)
        o = _hgrn_chunk(zq_ref[0, rows], zf_ref[0, rows], zi_ref[0, rows], llb_ref[...], l1m_ref[...],
                        oml_ref[...], st_ref, rev)
        if combine:
            o = o + ofw_ref[0, rows]
            zg = zg_ref[0, rows].astype(F32)
            y = jnp.concatenate([_rms(o[:, h * HG_DV:(h + 1) * HG_DV], gon_ref[...]) for h in range(HG_HEADS)],
                                axis=1)
            o = y * (zg * jax.nn.sigmoid(zg))
        o_ref[0, rows] = o.astype(o_ref.dtype)


def _hgrn2_mixer(z, lb_f, lb_b, g_onorm, chunks_pref=8):
    b, n, _ = z.shape
    chunks = chunks_pref if (n // HG_CHUNK) % chunks_pref == 0 else 1
    rows_blk = HG_CHUNK * chunks
    nblk = n // rows_blk

    def lb_rows(lbv):
        lbv = lbv.reshape(1, HG_W).astype(F32)
        return jnp.log(lbv), jnp.log1p(-lbv), 1.0 - lbv

    def spec(col, rev):
        if rev:
            return pl.BlockSpec((1, rows_blk, HG_W), lambda bi, i: (bi, nblk - 1 - i, col))
        return pl.BlockSpec((1, rows_blk, HG_W), lambda bi, i: (bi, i, col))

    vec = pl.BlockSpec((1, HG_W), lambda bi, i: (0, 0))
    scratch = [pltpu.VMEM((HG_HEADS, HG_DV, HG_DK), F32)]
    o_fw = pl.pallas_call(
        functools.partial(_hgrn_kernel, rev=False, combine=False, chunks=chunks),
        grid=(b, nblk),
        in_specs=[spec(OFF_HQ // HG_W, False), spec(OFF_HFF // HG_W, False), spec(OFF_HI // HG_W, False),
                  vec, vec, vec],
        out_specs=spec(0, False),
        out_shape=jax.ShapeDtypeStruct((b, n, HG_W), F32),
        scratch_shapes=scratch,
        compiler_params=_cparams("parallel", "arbitrary"),
        name="hgrn2_forward",
    )(z, z, z, *lb_rows(lb_f))
    return pl.pallas_call(
        functools.partial(_hgrn_kernel, rev=True, combine=True, chunks=chunks),
        grid=(b, nblk),
        in_specs=[spec(OFF_HQ // HG_W, True), spec(OFF_HFB // HG_W, True), spec(OFF_HI // HG_W, True),
                  spec(OFF_HG // HG_W, True), spec(0, True), vec, vec, vec,
                  pl.BlockSpec((1, HG_DV), lambda bi, i: (0, 0))],
        out_specs=spec(0, True),
        out_shape=jax.ShapeDtypeStruct((b, n, HG_W), BF16),
        scratch_shapes=scratch,
        compiler_params=_cparams("parallel", "arbitrary"),
        name="hgrn2_backward_combine",
    )(z, z, z, z, o_fw, *lb_rows(lb_b), g_onorm.reshape(1, HG_DV).astype(F32))


def _headnorm_rope(x, avg, g, cosv, sinv):
    hi = (x * x).astype(BF16)
    lo = (x * x - hi.astype(F32)).astype(BF16)
    ms = _dot(hi, avg) + _dot(lo, avg)
    y = x * lax.rsqrt(ms + EPS) * g
    lane = lax.broadcasted_iota(jnp.int32, y.shape, 1)
    first = (lane % ROPE_HALF) < (ROPE_HALF // 2)
    partner = jnp.where(first, pltpu.roll(y, LANES - ROPE_HALF // 2, 1), pltpu.roll(y, ROPE_HALF // 2, 1))
    return y * cosv + partner * sinv


def _attn_prep_kernel(zq_ref, zk_ref, zv_ref, cos_ref, sin_ref, gq_ref, gk_ref, avg_ref, qt_ref, k_ref, vt_ref):
    cosv = cos_ref[...]
    sinv = sin_ref[...]
    avg = avg_ref[...]
    for p in range(ATT_QW // LANES):
        sl = slice(p * LANES, (p + 1) * LANES)
        y = _headnorm_rope(zq_ref[0, :, sl].astype(F32), avg, gq_ref[...], cosv, sinv) * QK_SCALE
        qt_ref[0, sl, :] = y.T.astype(qt_ref.dtype)
    for p in range(ATT_KVW // LANES):
        sl = slice(p * LANES, (p + 1) * LANES)
        kn = _headnorm_rope(zk_ref[0, :, sl].astype(F32), avg, gk_ref[...], cosv, sinv)
        k_ref[0, :, sl] = kn.astype(k_ref.dtype)
        vt_ref[0, sl, :] = zv_ref[0, :, sl].astype(F32).T.astype(vt_ref.dtype)


def _flash_kernel(bound_ref, fast_ref, qt_ref, k_ref, vt_ref, o_ref, qpad_ref, m_ref, l_ref, acc_ref):
    kv = pl.program_id(2)
    tq = qt_ref.shape[2]
    hd = HEAD_DIM
    group = ATT_HEADS // ATT_KV_HEADS

    @pl.when(kv == 0)
    def _():
        zeros = jnp.zeros((hd, tq), qpad_ref.dtype)
        for h in range(ATT_HEADS):
            qh = qt_ref[0, h * hd:(h + 1) * hd, :]
            if (h // group) % 2 == 0:
                qpad_ref[h] = jnp.concatenate([qh, zeros], axis=0)
            else:
                qpad_ref[h] = jnp.concatenate([zeros, qh], axis=0)
        m_ref[...] = jnp.full_like(m_ref, -jnp.inf)
        l_ref[...] = jnp.zeros_like(l_ref)
        acc_ref[...] = jnp.zeros_like(acc_ref)

    def scores(h):
        a = h // (2 * group)
        return _dot(k_ref[0, :, a * LANES:(a + 1) * LANES], qpad_ref[h])

    def all_heads(update):
        st_next = scores(0)
        for h in range(ATT_HEADS):
            st = st_next
            if h + 1 < ATT_HEADS:
                st_next = scores(h + 1)
            g = h // group
            update(h, st, vt_ref[0, g * hd:(g + 1) * hd, :], slice(h * hd, (h + 1) * hd))

    def update_bounded(h, st, vt, rows):
        pt = jnp.exp2(st - bound_ref[0])
        l_ref[h] = l_ref[h] + jnp.sum(pt, axis=0, keepdims=True)
        acc_ref[rows] = acc_ref[rows] + _dot(vt, pt.astype(BF16))

    def update_running_max(h, st, vt, rows):
        m_prev = m_ref[h]
        m_new = jnp.maximum(m_prev, jnp.max(st, axis=0, keepdims=True))
        alpha = jnp.exp2(m_prev - m_new)
        pt = jnp.exp2(st - m_new)
        l_ref[h] = alpha * l_ref[h] + jnp.sum(pt, axis=0, keepdims=True)
        m_ref[h] = m_new
        acc_ref[rows] = acc_ref[rows] * alpha + _dot(vt, pt.astype(BF16))

    @pl.when(fast_ref[0] == 1)
    def _():
        all_heads(update_bounded)

    @pl.when(fast_ref[0] != 1)
    def _():
        all_heads(update_running_max)

    @pl.when(kv == pl.num_programs(2) - 1)
    def _():
        for h in range(ATT_HEADS):
            rows = slice(h * hd, (h + 1) * hd)
            acc_ref[rows] = acc_ref[rows] * (1.0 / l_ref[h])
        for p in range(ATT_QW // LANES):
            sl = slice(p * LANES, (p + 1) * LANES)
            o_ref[0, :, sl] = acc_ref[sl].T.astype(o_ref.dtype)


def _rope_tables(n):
    rows = n // GRID_W
    row = jnp.repeat(jnp.arange(rows, dtype=F32), GRID_W)
    colp = jnp.tile(jnp.arange(GRID_W, dtype=F32), rows)
    inv_freq = 1.0 / (ROPE_THETA ** (jnp.arange(0, ROPE_HALF, 2, dtype=F32) / ROPE_HALF))
    ang_r = row[:, None] * inv_freq
    ang_c = colp[:, None] * inv_freq
    cos64 = jnp.concatenate([jnp.cos(ang_r), jnp.cos(ang_r), jnp.cos(ang_c), jnp.cos(ang_c)], axis=-1)
    sin64 = jnp.concatenate([-jnp.sin(ang_r), jnp.sin(ang_r), -jnp.sin(ang_c), jnp.sin(ang_c)], axis=-1)
    return jnp.tile(cos64, (1, 2)), jnp.tile(sin64, (1, 2))


def _gqa_axial(z, g_q, g_k, cos_t, sin_t, tm_pref=512, tq_pref=1024, tk_pref=1024):
    b, n, _ = z.shape
    tm = _tile(n, tm_pref)
    seg = lax.broadcasted_iota(jnp.int32, (LANES, LANES), 0) // HEAD_DIM
    seg_t = lax.broadcasted_iota(jnp.int32, (LANES, LANES), 1) // HEAD_DIM
    avg = jnp.where(seg == seg_t, 1.0 / HEAD_DIM, 0.0).astype(BF16)
    gq2 = jnp.tile(g_q.reshape(1, HEAD_DIM).astype(F32), (1, 2))
    gk2 = jnp.tile(g_k.reshape(1, HEAD_DIM).astype(F32), (1, 2))
    vec = pl.BlockSpec((1, LANES), lambda bi, i: (0, 0))
    tab = pl.BlockSpec((tm, LANES), lambda bi, i: (i, 0))
    qt, kh, vt = pl.pallas_call(
        _attn_prep_kernel,
        grid=(b, n // tm),
        in_specs=[
            pl.BlockSpec((1, tm, ATT_QW), lambda bi, i: (bi, i, OFF_AQ // ATT_QW)),
            pl.BlockSpec((1, tm, ATT_KVW), lambda bi, i: (bi, i, OFF_AK // ATT_KVW)),
            pl.BlockSpec((1, tm, ATT_KVW), lambda bi, i: (bi, i, OFF_AV // ATT_KVW)),
            tab, tab, vec, vec,
            pl.BlockSpec((LANES, LANES), lambda bi, i: (0, 0)),
        ],
        out_specs=[
            pl.BlockSpec((1, ATT_QW, tm), lambda bi, i: (bi, 0, i)),
            pl.BlockSpec((1, tm, ATT_KVW), lambda bi, i: (bi, i, 0)),
            pl.BlockSpec((1, ATT_KVW, tm), lambda bi, i: (bi, 0, i)),
        ],
        out_shape=[
            jax.ShapeDtypeStruct((b, ATT_QW, n), BF16),
            jax.ShapeDtypeStruct((b, n, ATT_KVW), BF16),
            jax.ShapeDtypeStruct((b, ATT_KVW, n), BF16),
        ],
        compiler_params=_cparams("parallel", "parallel"),
        name="attn_prep",
    )(z, z, z, cos_t, sin_t, gq2, gk2, avg)

    bound = (SCORE_BOUND_SLACK * HEAD_DIM * QK_SCALE) * jnp.max(jnp.abs(g_q)) * jnp.max(jnp.abs(g_k))
    bound = bound.astype(F32).reshape(1)
    fast = (bound <= MAX_BOUNDED_SCORE).astype(jnp.int32)

    tq = _tile(n, tq_pref)
    tk = _tile(n, tk_pref)
    return pl.pallas_call(
        _flash_kernel,
        grid_spec=pltpu.PrefetchScalarGridSpec(
            num_scalar_prefetch=2,
            grid=(b, n // tq, n // tk),
            in_specs=[
                pl.BlockSpec((1, ATT_QW, tq), lambda bi, i, j, *_: (bi, 0, i)),
                pl.BlockSpec((1, tk, ATT_KVW), lambda bi, i, j, *_: (bi, j, 0)),
                pl.BlockSpec((1, ATT_KVW, tk), lambda bi, i, j, *_: (bi, 0, j)),
            ],
            out_specs=pl.BlockSpec((1, tq, ATT_QW), lambda bi, i, j, *_: (bi, i, 0)),
            scratch_shapes=[
                pltpu.VMEM((ATT_HEADS, LANES, tq), BF16),
                pltpu.VMEM((ATT_HEADS, 1, tq), F32),
                pltpu.VMEM((ATT_HEADS, 1, tq), F32),
                pltpu.VMEM((ATT_QW, tq), F32),
            ],
        ),
        out_shape=jax.ShapeDtypeStruct((b, n, ATT_QW), BF16),
        compiler_params=_cparams("parallel", "parallel", "arbitrary"),
        name="flash_gqa",
    )(bound, fast, qt, kh, vt)


def _merge_kernel(x_ref, yp_ref, yh_ref, ya_ref, ga_ref, gb_ref, wp_ref, wh_ref, wa_ref, wo_ref, o_ref):
    ga = jax.nn.sigmoid(ga_ref[...].astype(F32))
    gb = jax.nn.sigmoid(gb_ref[...].astype(F32))
    d = D_MODEL
    g_pool = ga[:, :d]
    g_hg = jnp.concatenate([ga[:, d:], gb[:, :2 * d - GATE_BLK]], axis=1)
    g_att = gb[:, 2 * d - GATE_BLK:]
    merged = (g_pool * _dot(yp_ref[...], wp_ref[...])
              + g_hg * _dot(yh_ref[...], wh_ref[...])
              + g_att * _dot(ya_ref[...], wa_ref[...]))
    o_ref[...] = x_ref[...] + _dot(merged.astype(BF16), wo_ref[...])


def _merge(x, y_pool, y_hg, y_att, z2, w_p, w_h, w_a, w_o, tm_pref=512):
    t, d = x.shape
    tm = _tile(t, tm_pref)

    def rowblk(w):
        return pl.BlockSpec((tm, w), lambda i: (i, 0))

    def whole(w):
        return pl.BlockSpec(w.shape, lambda i: (0, 0))

    return pl.pallas_call(
        _merge_kernel,
        grid=(t // tm,),
        in_specs=[
            rowblk(d), rowblk(POOL_WIDTH), rowblk(HG_W), rowblk(ATT_QW),
            pl.BlockSpec((tm, GATE_BLK), lambda i: (i, OFF_GATE // GATE_BLK)),
            pl.BlockSpec((tm, GATE_BLK), lambda i: (i, OFF_GATE // GATE_BLK + 1)),
            whole(w_p), whole(w_h), whole(w_a), whole(w_o),
        ],
        out_specs=rowblk(d),
        out_shape=jax.ShapeDtypeStruct((t, d), F32),
        compiler_params=_cparams("parallel"),
        name="merge_out_proj",
    )(x, y_pool, y_hg, y_att, z2, z2, w_p, w_h, w_a, w_o)


def _gelu_tanh(a):
    half = 0.5 * a
    inner = a * (0.7978845608028654 + (0.7978845608028654 * 0.044715) * (a * a))
    return half + half * jnp.tanh(inner)


def _ffn_kernel(x_ref, prev_ref, next_ref, g_ref, wu_ref, cw_ref, cb_ref, wd_ref, o_ref, h_ref, *, n, tm, tf):
    i = pl.program_id(1)
    hl = FFN_HALO
    g = g_ref[...]
    h_ref[hl:hl + tm] = _rms(x_ref[0], g).astype(BF16)
    h_ref[0:hl] = jnp.where(i > 0, _rms(prev_ref[0], g), 0.0).astype(BF16)
    h_ref[hl + tm:] = jnp.where((i + 1) * tm < n, _rms(next_ref[0], g), 0.0).astype(BF16)
    h = h_ref[...]
    nf = D_FF // tf

    def up(j):
        return tuple(_dot(h, wu_ref[:, c0 + j * tf:c0 + (j + 1) * tf]) for c0 in (0, D_FF))

    rows = tm + 2 * hl

    def conv(u, c0):
        cw = cw_ref[:, c0:c0 + tf]
        prev = pltpu.roll(u, 1, 0)[hl:hl + tm]
        nxt_ = pltpu.roll(u, rows - 1, 0)[hl:hl + tm]
        return prev * cw[0:1] + u[hl:hl + tm] * cw[1:2] + nxt_ * cw[2:3] + cb_ref[:, c0:c0 + tf]

    acc = None
    acts = []
    nxt = up(0)
    for j in range(nf):
        ua, ug = nxt
        if j + 1 < nf:
            nxt = up(j + 1)
        acts.append((_gelu_tanh(conv(ua, j * tf)) * conv(ug, D_FF + j * tf)).astype(BF16))
        if len(acts) == FFN_DOWN_GROUP or j == nf - 1:
            k0 = (j + 1 - len(acts)) * tf
            part = _dot(jnp.concatenate(acts, axis=1), wd_ref[k0:(j + 1) * tf, :])
            acc = part if acc is None else acc + part
            acts = []
    o_ref[0] = x_ref[0] + acc


def _ffn(x, g, w_up, conv_w, conv_b, w_down, tm_pref=512, tf_pref=256):
    b, n, d = x.shape
    tm = _tile(n, tm_pref)
    tf = _tile(D_FF, tf_pref)
    hb = tm // FFN_HALO
    nhb = n // FFN_HALO

    def whole(a):
        return pl.BlockSpec(a.shape, lambda bi, i: (0,) * a.ndim)

    return pl.pallas_call(
        functools.partial(_ffn_kernel, n=n, tm=tm, tf=tf),
        grid=(b, n // tm),
        in_specs=[
            pl.BlockSpec((1, tm, d), lambda bi, i: (bi, i, 0)),
            pl.BlockSpec((1, FFN_HALO, d), lambda bi, i: (bi, jnp.maximum(i * hb - 1, 0), 0)),
            pl.BlockSpec((1, FFN_HALO, d), lambda bi, i: (bi, jnp.minimum((i + 1) * hb, nhb - 1), 0)),
            whole(g), whole(w_up), whole(conv_w), whole(conv_b), whole(w_down),
        ],
        out_specs=pl.BlockSpec((1, tm, d), lambda bi, i: (bi, i, 0)),
        out_shape=jax.ShapeDtypeStruct((b, n, d), F32),
        scratch_shapes=[pltpu.VMEM((tm + 2 * FFN_HALO, d), BF16)],
        compiler_params=_cparams("parallel", "parallel"),
        name="conv_glu_ffn",
    )(x, x, x, g, w_up, conv_w, conv_b, w_down)


def _ple_kernel(x_ref, p_ref, g_ref, wg_ref, wp_ref, gf_ref, o_ref, *, final):
    x = x_ref[...]
    gate = jax.nn.sigmoid(_dot(_rms(x, g_ref[...]).astype(BF16), wg_ref[...]))
    y = x + gate * _dot(p_ref[0].astype(BF16), wp_ref[...])
    if final:
        y = _rms(y, gf_ref[...])
    o_ref[...] = y


def _ple(x, p, layer, g, w_gate, w_ple, g_final, final, tm_pref=512):
    t, d = x.shape
    tm = _tile(t, tm_pref)
    return pl.pallas_call(
        functools.partial(_ple_kernel, final=final),
        grid=(t // tm,),
        in_specs=[
            pl.BlockSpec((tm, d), lambda i: (i, 0)),
            pl.BlockSpec((1, tm, PLE_DIM), lambda i: (layer, i, 0)),
            pl.BlockSpec((1, d), lambda i: (0, 0)),
            pl.BlockSpec((d, d), lambda i: (0, 0)),
            pl.BlockSpec((PLE_DIM, d), lambda i: (0, 0)),
            pl.BlockSpec((1, d), lambda i: (0, 0)),
        ],
        out_specs=pl.BlockSpec((tm, d), lambda i: (i, 0)),
        out_shape=jax.ShapeDtypeStruct((t, d), F32),
        compiler_params=_cparams("parallel"),
        name="ple_final_norm" if final else "ple",
    )(x, p, g, w_gate, w_ple, g_final)


def _lower_bound(lb_raw):
    s = jnp.cumsum(jax.nn.softmax(lb_raw.astype(F32), axis=0), axis=0)
    return s - s[0:1]


def _trunk(x, p, wts):
    b, n, d = x.shape
    t = b * n
    depth = wts["w_in"].shape[0]
    cos_t, sin_t = _rope_tables(n)
    x2 = x.reshape(t, d)
    for i in range(depth):
        z2 = _norm_matmul(x2, wts["g_mix"][i], wts["w_in"][i])
        z = z2.reshape(b, n, IN_WIDTH)
        y_pool = _pool_mixer(z, wts["pool_w"][i], wts["pool_scale"][i])
        y_hg = _hgrn2_mixer(z, wts["lb_f"][i], wts["lb_b"][i], wts["hg_onorm"][i])
        y_att = _gqa_axial(z, wts["g_q"][i], wts["g_k"][i], cos_t, sin_t)
        x2 = _merge(x2, y_pool.reshape(t, POOL_WIDTH), y_hg.reshape(t, HG_W), y_att.reshape(t, ATT_QW), z2,
                    wts["w_br_pool"][i], wts["w_br_hg"][i], wts["w_br_att"][i], wts["w_out"][i])
        x2 = _ffn(x2.reshape(b, n, d), wts["g_ffn"][i], wts["w_up"][i], wts["conv_w"][i], wts["conv_b"][i],
                  wts["w_down"][i]).reshape(t, d)
        x2 = _ple(x2, p.reshape(depth, t, PLE_DIM), i, wts["g_ple"][i], wts["w_ple_gate"][i], wts["w_ple"][i],
                  wts["g_final"], final=(i == depth - 1))
    return x2.reshape(b, n, d)


def kernel(x_prompt, x_sample, p_prompt, p_sample, g_mix, w_in, pool_w, pool_scale, lb_raw_f, lb_raw_b, hg_onorm,
           g_q, g_k, w_br_pool, w_br_hg, w_br_att, w_out, g_ffn, w_up, conv_w, conv_b, w_down, g_ple, w_ple_gate,
           w_ple, g_final):
    depth = w_in.shape[0]

    def vec(a):
        return a.astype(F32).reshape(depth, 1, a.shape[-1])

    wts = dict(
        g_mix=vec(g_mix), w_in=w_in.astype(BF16), pool_w=pool_w.astype(BF16), pool_scale=vec(pool_scale),
        lb_f=_lower_bound(lb_raw_f), lb_b=_lower_bound(lb_raw_b), hg_onorm=hg_onorm, g_q=g_q, g_k=g_k,
        w_br_pool=w_br_pool.astype(BF16), w_br_hg=w_br_hg.astype(BF16), w_br_att=w_br_att.astype(BF16),
        w_out=w_out.astype(BF16), g_ffn=vec(g_ffn), w_up=w_up.astype(BF16), conv_w=conv_w.astype(F32),
        conv_b=vec(conv_b), w_down=w_down.astype(BF16), g_ple=vec(g_ple), w_ple_gate=w_ple_gate.astype(BF16),
        w_ple=w_ple.astype(BF16), g_final=g_final.astype(F32).reshape(1, -1),
    )
    return _trunk(x_prompt, p_prompt, wts), _trunk(x_sample, p_sample, wts)
```

```python
import functools

import jax
import jax.numpy as jnp
from jax import lax
from jax.experimental import pallas as pl
from jax.experimental.pallas import tpu as pltpu

F32 = jnp.float32
BF16 = jnp.bfloat16

EPS = 1e-6
D_MODEL = 1024
GRID_W = 64
PLE_DIM = 256
POOL_WIDTH = 512
POOL_GROUPS = 4
POOL_GDIM = POOL_WIDTH // POOL_GROUPS
POOL_HALF = (1, 2, 4, 8)
POOL_HALO = 8
HALO_BLK = 16
HG_HEADS = 4
HG_DK = 128
HG_DV = 128
HG_W = HG_HEADS * HG_DK
HG_CHUNK = 64
HG_SUB = 16
ATT_HEADS = 16
ATT_KV_HEADS = 4
HEAD_DIM = 64
ATT_QW = ATT_HEADS * HEAD_DIM
ATT_KVW = ATT_KV_HEADS * HEAD_DIM
ROPE_THETA = 10000.0
ROPE_HALF = HEAD_DIM // 2
D_FF = 2816
FFN_HALO = 8
FFN_DOWN_GROUP = 4

OFF_POOL = 0
OFF_HQ = 512
OFF_HFF = 1024
OFF_HFB = 1536
OFF_HI = 2048
OFF_HG = 2560
OFF_AQ = 3072
OFF_AK = 4096
OFF_AV = 4352
OFF_GATE = 4608
IN_WIDTH = 7680
GATE_BLK = 1536

LANES = 128
LOG2E = 1.4426950408889634
QK_SCALE = HEAD_DIM ** -0.5 * LOG2E
SCORE_BOUND_SLACK = 1.02
MAX_BOUNDED_SCORE = 30.0
VMEM_LIMIT = 56 * 1024 * 1024


def _cparams(*sem):
    return pltpu.CompilerParams(dimension_semantics=sem, vmem_limit_bytes=VMEM_LIMIT)


def _tile(n, pref):
    t = min(n, pref)
    while n % t:
        t //= 2
    return t


def _dot(a, b):
    return jnp.dot(a, b, preferred_element_type=F32)


def _dot_nt(a, b):
    return lax.dot_general(a, b, (((1,), (1,)), ((), ())), preferred_element_type=F32)


def _dot_tn(a, b):
    return lax.dot_general(a, b, (((0,), (0,)), ((), ())), preferred_element_type=F32)


def _rms(x, g):
    return x * lax.rsqrt(jnp.mean(x * x, axis=-1, keepdims=True) + EPS) * g


def _norm_matmul_kernel(x_ref, g_ref, w_ref, o_ref, *, tn):
    h = _rms(x_ref[...], g_ref[...]).astype(BF16)
    for j in range(w_ref.shape[1] // tn):
        sl = slice(j * tn, (j + 1) * tn)
        o_ref[:, sl] = _dot(h, w_ref[:, sl]).astype(o_ref.dtype)


def _norm_matmul(x, g, w, out_dtype=BF16, tm_pref=512, tn_pref=512):
    t, k = x.shape
    nout = w.shape[1]
    tm = _tile(t, tm_pref)
    tn = _tile(nout, tn_pref)
    return pl.pallas_call(
        functools.partial(_norm_matmul_kernel, tn=tn),
        grid=(t // tm,),
        in_specs=[
            pl.BlockSpec((tm, k), lambda i: (i, 0)),
            pl.BlockSpec((1, k), lambda i: (0, 0)),
            pl.BlockSpec((k, nout), lambda i: (0, 0)),
        ],
        out_specs=pl.BlockSpec((tm, nout), lambda i: (i, 0)),
        out_shape=jax.ShapeDtypeStruct((t, nout), out_dtype),
        compiler_params=_cparams("parallel"),
        name="norm_in_proj",
    )(x, g, w)


def _pool_kernel(cur_ref, prev_ref, next_ref, w_ref, scale_ref, o_ref, *, n, tn):
    i = pl.program_id(1)
    t0 = i * tn
    u = cur_ref[0].astype(F32)
    hl = POOL_HALO
    prev = jnp.where(i > 0, prev_ref[0, HALO_BLK - hl:].astype(F32), 0.0)
    nxt = jnp.where(t0 + tn < n, next_ref[0, :hl].astype(F32), 0.0)
    e = jnp.concatenate([prev, u, nxt], axis=0)
    s2 = e[0:-1] + e[1:]
    c1 = POOL_GDIM
    s4 = s2[0:-2, c1:] + s2[2:, c1:]
    s8 = s4[0:-4, c1:] + s4[4:, c1:]
    s16 = s8[0:-8, c1:] + s8[8:, c1:]
    sums = (s2[7:7 + tn, :c1], s4[6:6 + tn, :c1], s8[4:4 + tn, :c1], s16[0:tn, :])
    t = t0 + lax.broadcasted_iota(jnp.int32, (tn, 1), 0)
    outs = []
    for g in range(POOL_GROUPS):
        h = POOL_HALF[g]
        cnt = (jnp.minimum(t + h, n) - jnp.maximum(t - h, 0)).astype(F32)
        ug = u[:, g * c1:(g + 1) * c1]
        mixed = (sums[g] / cnt - ug).astype(BF16)
        outs.append(_dot(mixed, w_ref[g]))
    y = jnp.concatenate(outs, axis=1) * scale_ref[...]
    o_ref[0] = y.astype(o_ref.dtype)


def _pool_mixer(z, pool_w, pool_scale, tn_pref=512):
    b, n, _ = z.shape
    tn = _tile(n, tn_pref)
    hb = tn // HALO_BLK
    nhb = n // HALO_BLK
    return pl.pallas_call(
        functools.partial(_pool_kernel, n=n, tn=tn),
        grid=(b, n // tn),
        in_specs=[
            pl.BlockSpec((1, tn, POOL_WIDTH), lambda bi, i: (bi, i, OFF_POOL // POOL_WIDTH)),
            pl.BlockSpec((1, HALO_BLK, POOL_WIDTH),
                         lambda bi, i: (bi, jnp.maximum(i * hb - 1, 0), OFF_POOL // POOL_WIDTH)),
            pl.BlockSpec((1, HALO_BLK, POOL_WIDTH),
                         lambda bi, i: (bi, jnp.minimum((i + 1) * hb, nhb - 1), OFF_POOL // POOL_WIDTH)),
            pl.BlockSpec((POOL_GROUPS, POOL_GDIM, POOL_GDIM), lambda bi, i: (0, 0, 0)),
            pl.BlockSpec((1, POOL_WIDTH), lambda bi, i: (0, 0)),
        ],
        out_specs=pl.BlockSpec((1, tn, POOL_WIDTH), lambda bi, i: (bi, i, 0)),
        out_shape=jax.ShapeDtypeStruct((b, n, POOL_WIDTH), BF16),
        compiler_params=_cparams("parallel", "parallel"),
        name="pool_mixer",
    )(z, z, z, pool_w, pool_scale)


def _split3(x):
    hi = x.astype(BF16)
    r1 = x - hi.astype(F32)
    mid = r1.astype(BF16)
    lo = (r1 - mid.astype(F32)).astype(BF16)
    return hi, mid, lo


def _hgrn_chunk(zq, zf, v, log_lb, log1m_lb, one_m_lb, st_ref, rev):
    c = HG_CHUNK
    zq, zf, v = zq.astype(F32), zf.astype(F32), v.astype(F32)
    q = zq * (HG_DK ** -0.5)
    ez = jnp.exp(-jnp.abs(zf))
    log_sig = jnp.minimum(zf, 0.0) - jnp.log1p(ez)
    k = one_m_lb * (jnp.where(zf >= 0.0, ez, 1.0) / (1.0 + ez))
    ga = log_lb
    gb = log1m_lb + log_sig
    logf = jnp.maximum(ga, gb) + jnp.log1p(jnp.exp(-jnp.abs(ga - gb)))

    row = lax.broadcasted_iota(jnp.int32, (c, c), 0)
    col = lax.broadcasted_iota(jnp.int32, (c, c), 1)
    keep = (col >= row) if rev else (col <= row)
    tri = jnp.where(keep, 1.0, 0.0).astype(BF16)
    hi, mid, lo = _split3(logf)
    cum = (_dot(tri, hi) + _dot(tri, mid) + _dot(tri, lo)) * LOG2E
    last = cum[0:1] if rev else cum[c - 1:c]
    qhat = (q * jnp.exp2(cum)).astype(BF16)
    khat = (k * jnp.exp2(last - cum)).astype(BF16)
    dec = jnp.exp2(last)

    nsub = c // HG_SUB
    half = HG_SUB // 2
    heads = [slice(h * HG_DK, (h + 1) * HG_DK) for h in range(HG_HEADS)]
    a_rows = [[] for _ in range(HG_HEADS)]
    col_h = lax.broadcasted_iota(jnp.int32, (half, c), 1)
    for jb in range(nsub):
        r0 = jb * HG_SUB
        lo_r, hi_r = (r0 + HG_SUB, c) if rev else (0, r0)
        if hi_r > lo_r:
            ridx = lo_r if rev else hi_r - 1
            ref_row = cum[ridx:ridx + 1]
            qt = (q[r0:r0 + HG_SUB] * jnp.exp2(cum[r0:r0 + HG_SUB] - ref_row)).astype(BF16)
            pieces = [jnp.zeros((lo_r, HG_W), BF16),
                      (k[lo_r:hi_r] * jnp.exp2(ref_row - cum[lo_r:hi_r])).astype(BF16),
                      jnp.zeros((c - hi_r, HG_W), BF16)]
            kt = jnp.concatenate([p for p in pieces if p.shape[0]], axis=0)
            offs = [_dot_nt(qt[:, sl], kt[:, sl]) for sl in heads]
            halves = [[o[:half] for o in offs], [o[half:] for o in offs]]
        else:
            halves = [[jnp.zeros((half, c), F32) for _ in heads] for _ in range(2)]
        for l in range(HG_SUB):
            lr = r0 + l
            for which in range(2):
                if (which == 0 and not rev and l >= half) or (which == 1 and rev and l < half):
                    continue
                rr = r0 + which * half
                x = q[rr:rr + half] * jnp.exp2(jnp.minimum(cum[rr:rr + half] - cum[lr:lr + 1], 0.0)) * k[lr:lr + 1]
                for h, sl in enumerate(heads):
                    s = jnp.sum(x[:, sl], axis=-1, keepdims=True)
                    halves[which][h] = jnp.where(col_h == lr, s, halves[which][h])
        for h in range(HG_HEADS):
            a_rows[h] += [halves[0][h], halves[1][h]]

    outs = []
    for h, sl in enumerate(heads):
        a = jnp.where(keep, jnp.concatenate(a_rows[h], axis=0), 0.0).astype(BF16)
        vh = v[:, sl].astype(BF16)
        st = st_ref[h]
        outs.append(_dot(a, vh) + _dot_nt(qhat[:, sl], st.astype(BF16)))
        st_ref[h] = st * dec[:, sl] + _dot_tn(vh, khat[:, sl])
    return jnp.concatenate(outs, axis=1)


def _hgrn_kernel(*refs, rev, combine, chunks):
    if combine:
        zq_ref, zf_ref, zi_ref, zg_ref, ofw_ref, llb_ref, l1m_ref, oml_ref, gon_ref, o_ref, st_ref = refs
    else:
        zq_ref, zf_ref, zi_ref, llb_ref, l1m_ref, oml_ref, o_ref, st_ref = refs

    @pl.when(pl.program_id(1) == 0)
    def _():
        st_ref[...] = jnp.zeros_like(st_ref)

    order = range(chunks - 1, -1, -1) if rev else range(chunks)
    for ci in order:
        rows = slice(ci * HG_CHUNK, (ci + 1) * HG_CHUNK)
        o = _hgrn_chunk(zq_ref[0, rows], zf_ref[0, rows], zi_ref[0, rows], llb_ref[...], l1m_ref[...],
                        oml_ref[...], st_ref, rev)
        if combine:
            o = o + ofw_ref[0, rows]
            zg = zg_ref[0, rows].astype(F32)
            y = jnp.concatenate([_rms(o[:, h * HG_DV:(h + 1) * HG_DV], gon_ref[...]) for h in range(HG_HEADS)],
                                axis=1)
            o = y * (zg * jax.nn.sigmoid(zg))
        o_ref[0, rows] = o.astype(o_ref.dtype)


def _hgrn2_mixer(z, lb_f, lb_b, g_onorm, chunks_pref=16):
    b, n, _ = z.shape
    chunks = chunks_pref if (n // HG_CHUNK) % chunks_pref == 0 else 1
    rows_blk = HG_CHUNK * chunks
    nblk = n // rows_blk

    def lb_rows(lbv):
        lbv = lbv.reshape(1, HG_W).astype(F32)
        return jnp.log(lbv), jnp.log1p(-lbv), 1.0 - lbv

    def spec(col, rev):
        if rev:
            return pl.BlockSpec((1, rows_blk, HG_W), lambda bi, i: (bi, nblk - 1 - i, col))
        return pl.BlockSpec((1, rows_blk, HG_W), lambda bi, i: (bi, i, col))

    vec = pl.BlockSpec((1, HG_W), lambda bi, i: (0, 0))
    scratch = [pltpu.VMEM((HG_HEADS, HG_DV, HG_DK), F32)]
    o_fw = pl.pallas_call(
        functools.partial(_hgrn_kernel, rev=False, combine=False, chunks=chunks),
        grid=(b, nblk),
        in_specs=[spec(OFF_HQ // HG_W, False), spec(OFF_HFF // HG_W, False), spec(OFF_HI // HG_W, False),
                  vec, vec, vec],
        out_specs=spec(0, False),
        out_shape=jax.ShapeDtypeStruct((b, n, HG_W), F32),
        scratch_shapes=scratch,
        compiler_params=_cparams("parallel", "arbitrary"),
        name="hgrn2_forward",
    )(z, z, z, *lb_rows(lb_f))
    return pl.pallas_call(
        functools.partial(_hgrn_kernel, rev=True, combine=True, chunks=chunks),
        grid=(b, nblk),
        in_specs=[spec(OFF_HQ // HG_W, True), spec(OFF_HFB // HG_W, True), spec(OFF_HI // HG_W, True),
                  spec(OFF_HG // HG_W, True), spec(0, True), vec, vec, vec,
                  pl.BlockSpec((1, HG_DV), lambda bi, i: (0, 0))],
        out_specs=spec(0, True),
        out_shape=jax.ShapeDtypeStruct((b, n, HG_W), BF16),
        scratch_shapes=scratch,
        compiler_params=_cparams("parallel", "arbitrary"),
        name="hgrn2_backward_combine",
    )(z, z, z, z, o_fw, *lb_rows(lb_b), g_onorm.reshape(1, HG_DV).astype(F32))


def _headnorm_rope(x, avg, g, cosv, sinv):
    hi = (x * x).astype(BF16)
    lo = (x * x - hi.astype(F32)).astype(BF16)
    ms = _dot(hi, avg) + _dot(lo, avg)
    y = x * lax.rsqrt(ms + EPS) * g
    lane = lax.broadcasted_iota(jnp.int32, y.shape, 1)
    first = (lane % ROPE_HALF) < (ROPE_HALF // 2)
    partner = jnp.where(first, pltpu.roll(y, LANES - ROPE_HALF // 2, 1), pltpu.roll(y, ROPE_HALF // 2, 1))
    return y * cosv + partner * sinv


def _attn_prep_kernel(zq_ref, zk_ref, zv_ref, cos_ref, sin_ref, gq_ref, gk_ref, avg_ref, qt_ref, k_ref, vt_ref):
    cosv = cos_ref[...]
    sinv = sin_ref[...]
    avg = avg_ref[...]
    for p in range(ATT_QW // LANES):
        sl = slice(p * LANES, (p + 1) * LANES)
        y = _headnorm_rope(zq_ref[0, :, sl].astype(F32), avg, gq_ref[...], cosv, sinv) * QK_SCALE
        qt_ref[0, sl, :] = y.T.astype(qt_ref.dtype)
    for p in range(ATT_KVW // LANES):
        sl = slice(p * LANES, (p + 1) * LANES)
        kn = _headnorm_rope(zk_ref[0, :, sl].astype(F32), avg, gk_ref[...], cosv, sinv)
        k_ref[0, :, sl] = kn.astype(k_ref.dtype)
        vt_ref[0, sl, :] = zv_ref[0, :, sl].astype(F32).T.astype(vt_ref.dtype)


def _flash_kernel(bound_ref, fast_ref, qt_ref, k_ref, vt_ref, o_ref, qpad_ref, m_ref, l_ref, acc_ref):
    kv = pl.program_id(2)
    tq = qt_ref.shape[2]
    hd = HEAD_DIM
    group = ATT_HEADS // ATT_KV_HEADS

    @pl.when(kv == 0)
    def _():
        zeros = jnp.zeros((hd, tq), qpad_ref.dtype)
        for h in range(ATT_HEADS):
            qh = qt_ref[0, h * hd:(h + 1) * hd, :]
            if (h // group) % 2 == 0:
                qpad_ref[h] = jnp.concatenate([qh, zeros], axis=0)
            else:
                qpad_ref[h] = jnp.concatenate([zeros, qh], axis=0)
        m_ref[...] = jnp.full_like(m_ref, -jnp.inf)
        l_ref[...] = jnp.zeros_like(l_ref)
        acc_ref[...] = jnp.zeros_like(acc_ref)

    def scores(h):
        a = h // (2 * group)
        return _dot(k_ref[0, :, a * LANES:(a + 1) * LANES], qpad_ref[h])

    def all_heads(update):
        st_next = scores(0)
        for h in range(ATT_HEADS):
            st = st_next
            if h + 1 < ATT_HEADS:
                st_next = scores(h + 1)
            g = h // group
            update(h, st, vt_ref[0, g * hd:(g + 1) * hd, :], slice(h * hd, (h + 1) * hd))

    def update_bounded(h, st, vt, rows):
        pt = jnp.exp2(st - bound_ref[0])
        l_ref[h] = l_ref[h] + jnp.sum(pt, axis=0, keepdims=True)
        acc_ref[rows] = acc_ref[rows] + _dot(vt, pt.astype(BF16))

    def update_running_max(h, st, vt, rows):
        m_prev = m_ref[h]
        m_new = jnp.maximum(m_prev, jnp.max(st, axis=0, keepdims=True))
        alpha = jnp.exp2(m_prev - m_new)
        pt = jnp.exp2(st - m_new)
        l_ref[h] = alpha * l_ref[h] + jnp.sum(pt, axis=0, keepdims=True)
        m_ref[h] = m_new
        acc_ref[rows] = acc_ref[rows] * alpha + _dot(vt, pt.astype(BF16))

    @pl.when(fast_ref[0] == 1)
    def _():
        all_heads(update_bounded)

    @pl.when(fast_ref[0] != 1)
    def _():
        all_heads(update_running_max)

    @pl.when(kv == pl.num_programs(2) - 1)
    def _():
        for h in range(ATT_HEADS):
            rows = slice(h * hd, (h + 1) * hd)
            acc_ref[rows] = acc_ref[rows] * (1.0 / l_ref[h])
        for p in range(ATT_QW // LANES):
            sl = slice(p * LANES, (p + 1) * LANES)
            o_ref[0, :, sl] = acc_ref[sl].T.astype(o_ref.dtype)


def _rope_tables(n):
    rows = n // GRID_W
    row = jnp.repeat(jnp.arange(rows, dtype=F32), GRID_W)
    colp = jnp.tile(jnp.arange(GRID_W, dtype=F32), rows)
    inv_freq = 1.0 / (ROPE_THETA ** (jnp.arange(0, ROPE_HALF, 2, dtype=F32) / ROPE_HALF))
    ang_r = row[:, None] * inv_freq
    ang_c = colp[:, None] * inv_freq
    cos64 = jnp.concatenate([jnp.cos(ang_r), jnp.cos(ang_r), jnp.cos(ang_c), jnp.cos(ang_c)], axis=-1)
    sin64 = jnp.concatenate([-jnp.sin(ang_r), jnp.sin(ang_r), -jnp.sin(ang_c), jnp.sin(ang_c)], axis=-1)
    return jnp.tile(cos64, (1, 2)), jnp.tile(sin64, (1, 2))


def _gqa_axial(z, g_q, g_k, cos_t, sin_t, tm_pref=512, tq_pref=1024, tk_pref=1024):
    b, n, _ = z.shape
    tm = _tile(n, tm_pref)
    seg = lax.broadcasted_iota(jnp.int32, (LANES, LANES), 0) // HEAD_DIM
    seg_t = lax.broadcasted_iota(jnp.int32, (LANES, LANES), 1) // HEAD_DIM
    avg = jnp.where(seg == seg_t, 1.0 / HEAD_DIM, 0.0).astype(BF16)
    gq2 = jnp.tile(g_q.reshape(1, HEAD_DIM).astype(F32), (1, 2))
    gk2 = jnp.tile(g_k.reshape(1, HEAD_DIM).astype(F32), (1, 2))
    vec = pl.BlockSpec((1, LANES), lambda bi, i: (0, 0))
    tab = pl.BlockSpec((tm, LANES), lambda bi, i: (i, 0))
    qt, kh, vt = pl.pallas_call(
        _attn_prep_kernel,
        grid=(b, n // tm),
        in_specs=[
            pl.BlockSpec((1, tm, ATT_QW), lambda bi, i: (bi, i, OFF_AQ // ATT_QW)),
            pl.BlockSpec((1, tm, ATT_KVW), lambda bi, i: (bi, i, OFF_AK // ATT_KVW)),
            pl.BlockSpec((1, tm, ATT_KVW), lambda bi, i: (bi, i, OFF_AV // ATT_KVW)),
            tab, tab, vec, vec,
            pl.BlockSpec((LANES, LANES), lambda bi, i: (0, 0)),
        ],
        out_specs=[
            pl.BlockSpec((1, ATT_QW, tm), lambda bi, i: (bi, 0, i)),
            pl.BlockSpec((1, tm, ATT_KVW), lambda bi, i: (bi, i, 0)),
            pl.BlockSpec((1, ATT_KVW, tm), lambda bi, i: (bi, 0, i)),
        ],
        out_shape=[
            jax.ShapeDtypeStruct((b, ATT_QW, n), BF16),
            jax.ShapeDtypeStruct((b, n, ATT_KVW), BF16),
            jax.ShapeDtypeStruct((b, ATT_KVW, n), BF16),
        ],
        compiler_params=_cparams("parallel", "parallel"),
        name="attn_prep",
    )(z, z, z, cos_t, sin_t, gq2, gk2, avg)

    bound = (SCORE_BOUND_SLACK * HEAD_DIM * QK_SCALE) * jnp.max(jnp.abs(g_q)) * jnp.max(jnp.abs(g_k))
    bound = bound.astype(F32).reshape(1)
    fast = (bound <= MAX_BOUNDED_SCORE).astype(jnp.int32)

    tq = _tile(n, tq_pref)
    tk = _tile(n, tk_pref)
    return pl.pallas_call(
        _flash_kernel,
        grid_spec=pltpu.PrefetchScalarGridSpec(
            num_scalar_prefetch=2,
            grid=(b, n // tq, n // tk),
            in_specs=[
                pl.BlockSpec((1, ATT_QW, tq), lambda bi, i, j, *_: (bi, 0, i)),
                pl.BlockSpec((1, tk, ATT_KVW), lambda bi, i, j, *_: (bi, j, 0)),
                pl.BlockSpec((1, ATT_KVW, tk), lambda bi, i, j, *_: (bi, 0, j)),
            ],
            out_specs=pl.BlockSpec((1, tq, ATT_QW), lambda bi, i, j, *_: (bi, i, 0)),
            scratch_shapes=[
                pltpu.VMEM((ATT_HEADS, LANES, tq), BF16),
                pltpu.VMEM((ATT_HEADS, 1, tq), F32),
                pltpu.VMEM((ATT_HEADS, 1, tq), F32),
                pltpu.VMEM((ATT_QW, tq), F32),
            ],
        ),
        out_shape=jax.ShapeDtypeStruct((b, n, ATT_QW), BF16),
        compiler_params=_cparams("parallel", "parallel", "arbitrary"),
        name="flash_gqa",
    )(bound, fast, qt, kh, vt)


def _merge_kernel(x_ref, yp_ref, yh_ref, ya_ref, ga_ref, gb_ref, wp_ref, wh_ref, wa_ref, wo_ref, o_ref):
    ga = jax.nn.sigmoid(ga_ref[...].astype(F32))
    gb = jax.nn.sigmoid(gb_ref[...].astype(F32))
    d = D_MODEL
    g_pool = ga[:, :d]
    g_hg = jnp.concatenate([ga[:, d:], gb[:, :2 * d - GATE_BLK]], axis=1)
    g_att = gb[:, 2 * d - GATE_BLK:]
    merged = (g_pool * _dot(yp_ref[...], wp_ref[...])
              + g_hg * _dot(yh_ref[...], wh_ref[...])
              + g_att * _dot(ya_ref[...], wa_ref[...]))
    o_ref[...] = x_ref[...] + _dot(merged.astype(BF16), wo_ref[...])


def _merge(x, y_pool, y_hg, y_att, z2, w_p, w_h, w_a, w_o, tm_pref=512):
    t, d = x.shape
    tm = _tile(t, tm_pref)

    def rowblk(w):
        return pl.BlockSpec((tm, w), lambda i: (i, 0))

    def whole(w):
        return pl.BlockSpec(w.shape, lambda i: (0, 0))

    return pl.pallas_call(
        _merge_kernel,
        grid=(t // tm,),
        in_specs=[
            rowblk(d), rowblk(POOL_WIDTH), rowblk(HG_W), rowblk(ATT_QW),
            pl.BlockSpec((tm, GATE_BLK), lambda i: (i, OFF_GATE // GATE_BLK)),
            pl.BlockSpec((tm, GATE_BLK), lambda i: (i, OFF_GATE // GATE_BLK + 1)),
            whole(w_p), whole(w_h), whole(w_a), whole(w_o),
        ],
        out_specs=rowblk(d),
        out_shape=jax.ShapeDtypeStruct((t, d), F32),
        compiler_params=_cparams("parallel"),
        name="merge_out_proj",
    )(x, y_pool, y_hg, y_att, z2, z2, w_p, w_h, w_a, w_o)


def _gelu_tanh(a):
    half = 0.5 * a
    inner = a * (0.7978845608028654 + (0.7978845608028654 * 0.044715) * (a * a))
    return half + half * jnp.tanh(inner)


def _ffn_kernel(x_ref, prev_ref, next_ref, g_ref, wu_ref, cw_ref, cb_ref, wd_ref, o_ref, h_ref, *, n, tm, tf):
    i = pl.program_id(1)
    hl = FFN_HALO
    g = g_ref[...]
    h_ref[hl:hl + tm] = _rms(x_ref[0], g).astype(BF16)
    h_ref[0:hl] = jnp.where(i > 0, _rms(prev_ref[0], g), 0.0).astype(BF16)
    h_ref[hl + tm:] = jnp.where((i + 1) * tm < n, _rms(next_ref[0], g), 0.0).astype(BF16)
    h = h_ref[...]
    nf = D_FF // tf

    def up(j):
        return tuple(_dot(h, wu_ref[:, c0 + j * tf:c0 + (j + 1) * tf]) for c0 in (0, D_FF))

    rows = tm + 2 * hl

    def conv(u, c0):
        cw = cw_ref[:, c0:c0 + tf]
        prev = pltpu.roll(u, 1, 0)[hl:hl + tm]
        nxt_ = pltpu.roll(u, rows - 1, 0)[hl:hl + tm]
        return prev * cw[0:1] + u[hl:hl + tm] * cw[1:2] + nxt_ * cw[2:3] + cb_ref[:, c0:c0 + tf]

    acc = None
    acts = []
    nxt = up(0)
    for j in range(nf):
        ua, ug = nxt
        if j + 1 < nf:
            nxt = up(j + 1)
        acts.append((_gelu_tanh(conv(ua, j * tf)) * conv(ug, D_FF + j * tf)).astype(BF16))
        if len(acts) == FFN_DOWN_GROUP or j == nf - 1:
            k0 = (j + 1 - len(acts)) * tf
            part = _dot(jnp.concatenate(acts, axis=1), wd_ref[k0:(j + 1) * tf, :])
            acc = part if acc is None else acc + part
            acts = []
    o_ref[0] = x_ref[0] + acc


def _ffn(x, g, w_up, conv_w, conv_b, w_down, tm_pref=512, tf_pref=256):
    b, n, d = x.shape
    tm = _tile(n, tm_pref)
    tf = _tile(D_FF, tf_pref)
    hb = tm // FFN_HALO
    nhb = n // FFN_HALO

    def whole(a):
        return pl.BlockSpec(a.shape, lambda bi, i: (0,) * a.ndim)

    return pl.pallas_call(
        functools.partial(_ffn_kernel, n=n, tm=tm, tf=tf),
        grid=(b, n // tm),
        in_specs=[
            pl.BlockSpec((1, tm, d), lambda bi, i: (bi, i, 0)),
            pl.BlockSpec((1, FFN_HALO, d), lambda bi, i: (bi, jnp.maximum(i * hb - 1, 0), 0)),
            pl.BlockSpec((1, FFN_HALO, d), lambda bi, i: (bi, jnp.minimum((i + 1) * hb, nhb - 1), 0)),
            whole(g), whole(w_up), whole(conv_w), whole(conv_b), whole(w_down),
        ],
        out_specs=pl.BlockSpec((1, tm, d), lambda bi, i: (bi, i, 0)),
        out_shape=jax.ShapeDtypeStruct((b, n, d), F32),
        scratch_shapes=[pltpu.VMEM((tm + 2 * FFN_HALO, d), BF16)],
        compiler_params=_cparams("parallel", "parallel"),
        name="conv_glu_ffn",
    )(x, x, x, g, w_up, conv_w, conv_b, w_down)


def _ple_kernel(x_ref, p_ref, g_ref, wg_ref, wp_ref, gf_ref, o_ref, *, final):
    x = x_ref[...]
    gate = jax.nn.sigmoid(_dot(_rms(x, g_ref[...]).astype(BF16), wg_ref[...]))
    y = x + gate * _dot(p_ref[0].astype(BF16), wp_ref[...])
    if final:
        y = _rms(y, gf_ref[...])
    o_ref[...] = y


def _ple(x, p, layer, g, w_gate, w_ple, g_final, final, tm_pref=512):
    t, d = x.shape
    tm = _tile(t, tm_pref)
    return pl.pallas_call(
        functools.partial(_ple_kernel, final=final),
        grid=(t // tm,),
        in_specs=[
            pl.BlockSpec((tm, d), lambda i: (i, 0)),
            pl.BlockSpec((1, tm, PLE_DIM), lambda i: (layer, i, 0)),
            pl.BlockSpec((1, d), lambda i: (0, 0)),
            pl.BlockSpec((d, d), lambda i: (0, 0)),
            pl.BlockSpec((PLE_DIM, d), lambda i: (0, 0)),
            pl.BlockSpec((1, d), lambda i: (0, 0)),
        ],
        out_specs=pl.BlockSpec((tm, d), lambda i: (i, 0)),
        out_shape=jax.ShapeDtypeStruct((t, d), F32),
        compiler_params=_cparams("parallel"),
        name="ple_final_norm" if final else "ple",
    )(x, p, g, w_gate, w_ple, g_final)


def _lower_bound(lb_raw):
    s = jnp.cumsum(jax.nn.softmax(lb_raw.astype(F32), axis=0), axis=0)
    return s - s[0:1]


def _trunk(x, p, wts):
    b, n, d = x.shape
    t = b * n
    depth = wts["w_in"].shape[0]
    cos_t, sin_t = _rope_tables(n)
    x2 = x.reshape(t, d)
    for i in range(depth):
        z2 = _norm_matmul(x2, wts["g_mix"][i], wts["w_in"][i])
        z = z2.reshape(b, n, IN_WIDTH)
        y_pool = _pool_mixer(z, wts["pool_w"][i], wts["pool_scale"][i])
        y_hg = _hgrn2_mixer(z, wts["lb_f"][i], wts["lb_b"][i], wts["hg_onorm"][i])
        y_att = _gqa_axial(z, wts["g_q"][i], wts["g_k"][i], cos_t, sin_t)
        x2 = _merge(x2, y_pool.reshape(t, POOL_WIDTH), y_hg.reshape(t, HG_W), y_att.reshape(t, ATT_QW), z2,
                    wts["w_br_pool"][i], wts["w_br_hg"][i], wts["w_br_att"][i], wts["w_out"][i])
        x2 = _ffn(x2.reshape(b, n, d), wts["g_ffn"][i], wts["w_up"][i], wts["conv_w"][i], wts["conv_b"][i],
                  wts["w_down"][i]).reshape(t, d)
        x2 = _ple(x2, p.reshape(depth, t, PLE_DIM), i, wts["g_ple"][i], wts["w_ple_gate"][i], wts["w_ple"][i],
                  wts["g_final"], final=(i == depth - 1))
    return x2.reshape(b, n, d)


def kernel(x_prompt, x_sample, p_prompt, p_sample, g_mix, w_in, pool_w, pool_scale, lb_raw_f, lb_raw_b, hg_onorm,
           g_q, g_k, w_br_pool, w_br_hg, w_br_att, w_out, g_ffn, w_up, conv_w, conv_b, w_down, g_ple, w_ple_gate,
           w_ple, g_final):
    depth = w_in.shape[0]

    def vec(a):
        return a.astype(F32).reshape(depth, 1, a.shape[-1])

    wts = dict(
        g_mix=vec(g_mix), w_in=w_in.astype(BF16), pool_w=pool_w.astype(BF16), pool_scale=vec(pool_scale),
        lb_f=_lower_bound(lb_raw_f), lb_b=_lower_bound(lb_raw_b), hg_onorm=hg_onorm, g_q=g_q, g_k=g_k,
        w_br_pool=w_br_pool.astype(BF16), w_br_hg=w_br_hg.astype(BF16), w_br_att=w_br_att.astype(BF16),
        w_out=w_out.astype(BF16), g_ffn=vec(g_ffn), w_up=w_up.astype(BF16), conv_w=conv_w.astype(F32),
        conv_b=vec(conv_b), w_down=w_down.astype(BF16), g_ple=vec(g_ple), w_ple_gate=w_ple_gate.astype(BF16),
        w_ple=w_ple.astype(BF16), g_final=g_final.astype(F32).reshape(1, -1),
    )
    return _trunk(x_prompt, p_prompt, wts), _trunk(x_sample, p_sample, wts)
```
